```python
import jax, jax.numpy as jnp
from jax import lax
import numpy as np

D_MODEL = 4096
BATCH = 4
SEQ = 4096
DEPTH = 1

HEAD_DIM = 128
MOBA_HEADS = 16
MOBA_BLOCK = 256
MOBA_TOPK = 3
MOBA_QCHUNK = 32
DSA_HEADS = 16
DSA_KV_HEADS = 4
DSA_TOPK_MAX = 256
DSA_QCHUNK = 128
IDX_HEADS = 32
IDX_DIM = 64
N_EXPERTS = 64
EXPERT_TOPK = 8
N_GROUPS = 8
TOPK_GROUPS = 4
D_EXPERT = 512
D_SHARED = 1024
ROUTED_SCALE = 2.5
MOE_ROW_CHUNK = 256
NORM_EPS = 1e-6

MOBA_WIDTH = MOBA_HEADS * HEAD_DIM
DSA_Q_WIDTH = DSA_HEADS * HEAD_DIM
DSA_KV_WIDTH = DSA_KV_HEADS * HEAD_DIM
IDX_Q_WIDTH = IDX_HEADS * IDX_DIM
IN_SPLITS = (MOBA_WIDTH, MOBA_WIDTH, MOBA_WIDTH, DSA_Q_WIDTH, DSA_KV_WIDTH, DSA_KV_WIDTH,
             IDX_Q_WIDTH, IDX_DIM, IDX_HEADS, D_MODEL, D_MODEL)
D_IN = sum(IN_SPLITS)

kernel_name = "hybrid_moba_dsa_moe_adaln"


def rms_norm(x, gain):
    xf = x.astype(jnp.float32)
    y = xf * lax.rsqrt(jnp.mean(xf * xf, axis=-1, keepdims=True) + NORM_EPS)
    return (y * gain.astype(jnp.float32)).astype(x.dtype)


def alibi_slopes(n_heads):
    return 2.0 ** (-8.0 * jnp.arange(1, n_heads + 1, dtype=jnp.float32) / n_heads)


def moba_attention(q, k, v):
    B, S, H, Dh = q.shape
    nb = -(-S // MOBA_BLOCK)
    s_pad = nb * MOBA_BLOCK
    n_gate = min(MOBA_TOPK, nb)
    pad = ((0, 0), (0, s_pad - S), (0, 0), (0, 0))
    qh = q.transpose(0, 2, 1, 3)
    kh = jnp.pad(k, pad).transpose(0, 2, 1, 3)
    vh = jnp.pad(v, pad).transpose(0, 2, 1, 3)
    k_blocks = kh.reshape(B, H, nb, MOBA_BLOCK, Dh)
    v_blocks = vh.reshape(B, H, nb, MOBA_BLOCK, Dh)
    k_mean = jnp.mean(k_blocks.astype(jnp.float32), axis=3)
    slopes = alibi_slopes(H)[None, :, None, None]
    scale = Dh ** -0.5
    b_ix = jnp.arange(B)[:, None, None]
    h_ix = jnp.arange(H)[None, :, None]
    blk_offs = jnp.arange(MOBA_BLOCK)

    def chunk(ci):
        t0 = ci * MOBA_QCHUNK
        own = t0 // MOBA_BLOCK
        t = t0 + jnp.arange(MOBA_QCHUNK)
        qc = lax.dynamic_slice_in_dim(qh, t0, MOBA_QCHUNK, axis=2)
        gate = jnp.einsum('bhqd,bhnd->bhqn', qc.astype(jnp.float32), k_mean)
        gate = jnp.where(jnp.arange(nb) < own, gate, -jnp.inf)
        _, sel = lax.top_k(gate, n_gate)
        sel_ok = sel < own
        k_own = lax.dynamic_slice_in_dim(kh, own * MOBA_BLOCK, MOBA_BLOCK, axis=2)
        v_own = lax.dynamic_slice_in_dim(vh, own * MOBA_BLOCK, MOBA_BLOCK, axis=2)
        s_own = own * MOBA_BLOCK + blk_offs
        lg_own = (jnp.einsum('bhqd,bhkd->bhqk', qc, k_own).astype(jnp.float32) * scale
                  - slopes * (t[:, None] - s_own[None, :]).astype(jnp.float32))
        logits = [jnp.where(s_own[None, :] <= t[:, None], lg_own, -jnp.inf)]
        for j in range(n_gate):
            idx = sel[..., j]
            k_j = k_blocks[b_ix, h_ix, idx]
            s_j = idx[..., None] * MOBA_BLOCK + blk_offs
            lg = (jnp.einsum('bhqd,bhqkd->bhqk', qc, k_j).astype(jnp.float32) * scale
                  - slopes * (t[None, None, :, None] - s_j).astype(jnp.float32))
            logits.append(jnp.where(sel_ok[..., j, None], lg, -jnp.inf))
        p = jax.nn.softmax(jnp.concatenate(logits, axis=-1), axis=-1).astype(v.dtype)
        out = jnp.einsum('bhqk,bhkd->bhqd', p[..., :MOBA_BLOCK], v_own)
        for j in range(n_gate):
            v_j = v_blocks[b_ix, h_ix, sel[..., j]]
            p_j = p[..., (j + 1) * MOBA_BLOCK:(j + 2) * MOBA_BLOCK]
            out = out + jnp.einsum('bhqk,bhqkd->bhqd', p_j, v_j)
        return out

    outs = lax.map(chunk, jnp.arange(S // MOBA_QCHUNK))
    return outs.transpose(1, 0, 3, 2, 4).reshape(B, S, H * Dh)


def dsa_attention(q, k, v, q_idx, k_idx, w_idx):
    B, S, H, Dh = q.shape
    Hkv = k.shape[2]
    G = H // Hkv
    n_sel = min(DSA_TOPK_MAX, S // 4)
    slopes = alibi_slopes(H).reshape(Hkv, G)[None, None, :, :, None]
    scale = Dh ** -0.5
    idx_scale = IDX_DIM ** -0.5
    b_ix = jnp.arange(B)[:, None, None]
    s_all = jnp.arange(S)

    def chunk(ci):
        t0 = ci * DSA_QCHUNK
        t = t0 + jnp.arange(DSA_QCHUNK)
        qi = lax.dynamic_slice_in_dim(q_idx, t0, DSA_QCHUNK, axis=1)
        wi = lax.dynamic_slice_in_dim(w_idx, t0, DSA_QCHUNK, axis=1)
        dots = jnp.einsum('bqhd,bsd->bqhs', qi, k_idx).astype(jnp.float32) * idx_scale
        score = jnp.einsum('bqh,bqhs->bqs', wi.astype(jnp.float32), jax.nn.relu(dots))
        score = jnp.where(s_all[None, :] <= t[:, None], score, -jnp.inf)
        _, sel = lax.top_k(score, n_sel)
        sel_ok = sel <= t[None, :, None]
        k_sel = k[b_ix, sel]
        v_sel = v[b_ix, sel]
        qc = lax.dynamic_slice_in_dim(q, t0, DSA_QCHUNK, axis=1).reshape(B, DSA_QCHUNK, Hkv, G, Dh)
        lg = jnp.einsum('bqhgd,bqkhd->bqhgk', qc, k_sel).astype(jnp.float32) * scale
        dist = (t[None, :, None] - sel).astype(jnp.float32)
        lg = lg - slopes * dist[:, :, None, None, :]
        lg = jnp.where(sel_ok[:, :, None, None, :], lg, -jnp.inf)
        p = jax.nn.softmax(lg, axis=-1).astype(v.dtype)
        o = jnp.einsum('bqhgk,bqkhd->bqhgd', p, v_sel)
        return o.reshape(B, DSA_QCHUNK, H * Dh)

    outs = lax.map(chunk, jnp.arange(S // DSA_QCHUNK))
    return outs.transpose(1, 0, 2, 3).reshape(B, S, H * Dh)


def swiglu(h, w1, w3, w2):
    return (jax.nn.silu(h @ w1) * (h @ w3)) @ w2


def route(h, w_router, router_bias):
    T = h.shape[0]
    per_group = N_EXPERTS // N_GROUPS
    scores = jax.nn.sigmoid((h @ w_router).astype(jnp.float32))
    choice = scores + router_bias.astype(jnp.float32)
    grp_score = lax.top_k(choice.reshape(T, N_GROUPS, per_group), 2)[0].sum(-1)
    _, top_grp = lax.top_k(grp_score, TOPK_GROUPS)
    grp_mask = jnp.any(top_grp[:, :, None] == jnp.arange(N_GROUPS)[None, None, :], axis=1)
    choice = jnp.where(jnp.repeat(grp_mask, per_group, axis=1), choice, -jnp.inf)
    _, idx = lax.top_k(choice, EXPERT_TOPK)
    w = jnp.take_along_axis(scores, idx, axis=1)
    w = w / jnp.sum(w, axis=-1, keepdims=True) * ROUTED_SCALE
    return idx, w


def routed_experts(h, idx, w, w1, w3, w2):
    T, D = h.shape
    E = w1.shape[0]
    M = T * EXPERT_TOPK
    C = MOE_ROW_CHUNK
    P = M + E * C
    n_chunks = P // C
    e_flat = idx.reshape(M)
    tok_flat = jnp.repeat(jnp.arange(T, dtype=jnp.int32), EXPERT_TOPK)
    w_flat = w.reshape(M)
    order = jnp.argsort(e_flat)
    e_sorted = e_flat[order]
    counts = jnp.bincount(e_flat, length=E)
    starts = jnp.cumsum(counts) - counts
    padded = (counts + C - 1) // C * C
    pends = jnp.cumsum(padded)
    pstarts = pends - padded
    pos = pstarts[e_sorted] + jnp.arange(M) - starts[e_sorted]
    tok_pad = jnp.zeros((P,), jnp.int32).at[pos].set(tok_flat[order])
    w_pad = jnp.zeros((P,), jnp.float32).at[pos].set(w_flat[order])
    c_start = jnp.arange(n_chunks) * C
    e_chunk = jnp.minimum(jnp.searchsorted(pends, c_start, side='right'), E - 1)
    active = c_start < pends[-1]

    def body(y, xs):
        tok, wt, e, act = xs

        def compute(y):
            out = swiglu(h[tok], w1[e], w3[e], w2[e]) * wt[:, None].astype(h.dtype)
            return y.at[tok].add(out)

        return lax.cond(act, compute, lambda y: y, y), None

    y, _ = lax.scan(body, jnp.zeros_like(h),
                    (tok_pad.reshape(n_chunks, C), w_pad.reshape(n_chunks, C), e_chunk, active))
    return y


def setup_inputs(seed: int = 0) -> dict:
    key = jax.random.key(seed)
    ks = jax.random.split(key, 23)
    n = jax.random.normal
    f32 = jnp.float32
    L = DEPTH
    return {
        "x": n(ks[0], (BATCH, SEQ, D_MODEL), f32),
        "c": n(ks[1], (BATCH, D_MODEL), f32),
        "ln1_gain": 1.0 + 0.02 * n(ks[2], (L, D_MODEL), f32),
        "ln2_gain": 1.0 + 0.02 * n(ks[3], (L, D_MODEL), f32),
        "w_ada": 0.5 * D_MODEL ** -0.5 * n(ks[4], (L, D_MODEL, 6 * D_MODEL), f32),
        "b_ada": 0.01 * n(ks[5], (L, 6 * D_MODEL), f32),
        "w_in": D_MODEL ** -0.5 * n(ks[6], (L, D_MODEL, D_IN), f32),
        "moba_q_gain": 1.0 + 0.02 * n(ks[7], (L, HEAD_DIM), f32),
        "moba_k_gain": 1.0 + 0.02 * n(ks[8], (L, HEAD_DIM), f32),
        "dsa_q_gain": 1.0 + 0.02 * n(ks[9], (L, HEAD_DIM), f32),
        "dsa_k_gain": 1.0 + 0.02 * n(ks[10], (L, HEAD_DIM), f32),
        "idx_k_gain": 1.0 + 0.02 * n(ks[11], (L, IDX_DIM), f32),
        "w_proj_moba": MOBA_WIDTH ** -0.5 * n(ks[12], (L, MOBA_WIDTH, D_MODEL), f32),
        "w_proj_dsa": DSA_Q_WIDTH ** -0.5 * n(ks[13], (L, DSA_Q_WIDTH, D_MODEL), f32),
        "w_out": D_MODEL ** -0.5 * n(ks[14], (L, D_MODEL, D_MODEL), f32),
        "w_router": D_MODEL ** -0.5 * n(ks[15], (L, D_MODEL, N_EXPERTS), f32),
        "router_bias": 0.01 * n(ks[16], (L, N_EXPERTS), f32),
        "w1_experts": D_MODEL ** -0.5 * n(ks[17], (L, N_EXPERTS, D_MODEL, D_EXPERT), f32),
        "w3_experts": D_MODEL ** -0.5 * n(ks[18], (L, N_EXPERTS, D_MODEL, D_EXPERT), f32),
        "w2_experts": D_EXPERT ** -0.5 * n(ks[19], (L, N_EXPERTS, D_EXPERT, D_MODEL), f32),
        "w1_shared": D_MODEL ** -0.5 * n(ks[20], (L, D_MODEL, D_SHARED), f32),
        "w3_shared": D_MODEL ** -0.5 * n(ks[21], (L, D_MODEL, D_SHARED), f32),
        "w2_shared": D_SHARED ** -0.5 * n(ks[22], (L, D_SHARED, D_MODEL), f32),
    }


def reference(x, c, ln1_gain, ln2_gain, w_ada, b_ada, w_in, moba_q_gain, moba_k_gain, dsa_q_gain,
              dsa_k_gain, idx_k_gain, w_proj_moba, w_proj_dsa, w_out, w_router, router_bias,
              w1_experts, w3_experts, w2_experts, w1_shared, w3_shared, w2_shared):
    B, S, D = x.shape
    split_points = np.cumsum(IN_SPLITS)[:-1].tolist()
    for l in range(DEPTH):
        mod = jax.nn.silu(c) @ w_ada[l] + b_ada[l]
        sh1, sc1, g1, sh2, sc2, g2 = jnp.split(mod, 6, axis=-1)

        h = rms_norm(x, ln1_gain[l]) * (1 + sc1[:, None, :]) + sh1[:, None, :]
        proj = h @ w_in[l]
        q_a, k_a, v_a, q_b, k_b, v_b, q_i, k_i, w_i, gate_a, gate_b = jnp.split(proj, split_points, axis=-1)

        q_a = rms_norm(q_a.reshape(B, S, MOBA_HEADS, HEAD_DIM), moba_q_gain[l])
        k_a = rms_norm(k_a.reshape(B, S, MOBA_HEADS, HEAD_DIM), moba_k_gain[l])
        v_a = v_a.reshape(B, S, MOBA_HEADS, HEAD_DIM)
        o_a = moba_attention(q_a, k_a, v_a)

        q_b = rms_norm(q_b.reshape(B, S, DSA_HEADS, HEAD_DIM), dsa_q_gain[l])
        k_b = rms_norm(k_b.reshape(B, S, DSA_KV_HEADS, HEAD_DIM), dsa_k_gain[l])
        v_b = v_b.reshape(B, S, DSA_KV_HEADS, HEAD_DIM)
        q_i = q_i.reshape(B, S, IDX_HEADS, IDX_DIM)
        k_i = rms_norm(k_i, idx_k_gain[l])
        w_i = w_i * (IDX_HEADS ** -0.5)
        o_b = dsa_attention(q_b, k_b, v_b, q_i, k_i, w_i)

        merged = (jax.nn.sigmoid(gate_a) * (o_a @ w_proj_moba[l])
                  + jax.nn.sigmoid(gate_b) * (o_b @ w_proj_dsa[l]))
        x = x + g1[:, None, :] * (merged @ w_out[l])

        h2 = rms_norm(x, ln2_gain[l]) * (1 + sc2[:, None, :]) + sh2[:, None, :]
        hf = h2.reshape(B * S, D)
        e_idx, e_w = route(hf, w_router[l], router_bias[l])
        y = (swiglu(hf, w1_shared[l], w3_shared[l], w2_shared[l])
             + routed_experts(hf, e_idx, e_w, w1_experts[l], w3_experts[l], w2_experts[l]))
        x = x + g2[:, None, :] * y.reshape(B, S, D)
    return x
```

```python
import functools

import jax
import jax.numpy as jnp
from jax import lax
from jax.experimental import pallas as pl
from jax.experimental.pallas import tpu as pltpu

F32 = jnp.float32
BF16 = jnp.bfloat16
I32 = jnp.int32

HEAD_DIM = 128
MOBA_HEADS = 16
MOBA_BLOCK = 256
MOBA_TOPK = 3
DSA_HEADS = 16
DSA_KV_HEADS = 4
DSA_TOPK_MAX = 256
IDX_HEADS = 32
IDX_DIM = 64
N_EXPERTS = 64
EXPERT_TOPK = 8
N_GROUPS = 8
TOPK_GROUPS = 4
ROUTED_SCALE = 2.5
NORM_EPS = 1e-6

LANES = 128
V7X_VMEM_LIMIT_BYTES = 56 * 1024 * 1024

MOE_CHUNK = 256
NEG = -1e30
INT_MIN = -2 ** 31

_NT = (((1,), (1,)), ((), ()))


def _cparams(sem):
    return pltpu.CompilerParams(dimension_semantics=sem, vmem_limit_bytes=V7X_VMEM_LIMIT_BYTES)


def _tile(n, pref):
    if n <= pref:
        return n
    t = pref - pref % LANES
    while t > LANES and n % t:
        t -= LANES
    assert n % t == 0, (n, pref)
    return t


def _ada_kernel(c_ref, w_ref, b_ref, o_ref):
    c = c_ref[...]
    s = c * jax.nn.sigmoid(c)
    acc = jnp.dot(s.astype(BF16), w_ref[...].astype(BF16), preferred_element_type=F32)
    o_ref[...] = acc + b_ref[...]


def _ada(c_pad, w_ada, b_ada):
    rows, d = c_pad.shape
    n = w_ada.shape[1]
    tn = _tile(n, 512)
    return pl.pallas_call(
        _ada_kernel,
        grid=(n // tn,),
        in_specs=[pl.BlockSpec((rows, d), lambda j: (0, 0)),
                  pl.BlockSpec((d, tn), lambda j: (0, j)),
                  pl.BlockSpec((1, tn), lambda j: (0, j))],
        out_specs=pl.BlockSpec((rows, tn), lambda j: (0, j)),
        out_shape=jax.ShapeDtypeStruct((rows, n), F32),
        compiler_params=_cparams(("arbitrary",)),
        name="ada_mod",
    )(c_pad, w_ada, b_ada)


def _normmod_kernel(x_ref, gain_ref, sc_ref, sh_ref, *o_refs):
    x = x_ref[0]
    y = x * lax.rsqrt(jnp.mean(x * x, axis=-1, keepdims=True) + NORM_EPS) * gain_ref[...]
    h = y * (1.0 + sc_ref[0]) + sh_ref[0]
    o_refs[0][0] = h.astype(BF16)
    if len(o_refs) > 1:
        o_refs[1][0] = h


def _normmod(x, gain, sc, sh, with_f32):
    b, s, d = x.shape
    ts = _tile(s, 256)
    blk = pl.BlockSpec((1, ts, d), lambda bi, i: (bi, i, 0))
    vec = pl.BlockSpec((1, 1, d), lambda bi, i: (bi, 0, 0))
    out_shape = [jax.ShapeDtypeStruct((b, s, d), BF16)]
    out_specs = [blk]
    if with_f32:
        out_shape.append(jax.ShapeDtypeStruct((b, s, d), F32))
        out_specs.append(blk)
    return pl.pallas_call(
        _normmod_kernel,
        grid=(b, s // ts),
        in_specs=[blk, pl.BlockSpec((1, d), lambda bi, i: (0, 0)), vec, vec],
        out_specs=out_specs,
        out_shape=out_shape,
        compiler_params=_cparams(("arbitrary", "arbitrary")),
        name="norm_modulate",
    )(x, gain.reshape(1, d), sc.reshape(b, 1, d), sh.reshape(b, 1, d))


def _proj_kernel(a_ref, w_ref, aux_ref, o_ref, *, mode):
    acc = jnp.dot(a_ref[0], w_ref[...], preferred_element_type=F32)
    tn = acc.shape[1]
    if mode == "plain":
        o_ref[0] = acc.astype(o_ref.dtype)
    elif mode == "sigmoid":
        o_ref[0] = jax.nn.sigmoid(acc).astype(o_ref.dtype)
    elif mode == "headnorm":
        for g in range(tn // HEAD_DIM):
            sl = slice(g * HEAD_DIM, (g + 1) * HEAD_DIM)
            blk = acc[:, sl]
            ms = jnp.mean(blk * blk, axis=-1, keepdims=True)
            o_ref[0, :, sl] = (blk * lax.rsqrt(ms + NORM_EPS) * aux_ref[:, sl]).astype(o_ref.dtype)
    elif mode == "indexer":
        lane = lax.broadcasted_iota(I32, acc.shape, 1)
        is_key = lane < IDX_DIM
        ms = jnp.sum(jnp.where(is_key, acc * acc, 0.0), axis=-1, keepdims=True) / IDX_DIM
        o_ref[0] = acc * jnp.where(is_key, lax.rsqrt(ms + NORM_EPS), 1.0) * aux_ref[...]
    else:
        raise ValueError(mode)


def _proj(a, w, aux, mode, out_dtype, tm_pref=1024, tn_pref=512):
    b, s, k = a.shape
    n = w.shape[1]
    tm, tn = _tile(s, tm_pref), _tile(n, tn_pref)
    return pl.pallas_call(
        functools.partial(_proj_kernel, mode=mode),
        grid=(b, s // tm, n // tn),
        in_specs=[pl.BlockSpec((1, tm, k), lambda bi, i, j: (bi, i, 0)),
                  pl.BlockSpec((k, tn), lambda bi, i, j: (0, j)),
                  pl.BlockSpec((1, tn), lambda bi, i, j: (0, j))],
        out_specs=pl.BlockSpec((1, tm, tn), lambda bi, i, j: (bi, i, j)),
        out_shape=jax.ShapeDtypeStruct((b, s, n), out_dtype),
        compiler_params=_cparams(("arbitrary", "arbitrary", "arbitrary")),
        name="proj_" + mode,
    )(a, w, aux)


def _moba_kernel(slope_ref, q_ref, k_ref, v_ref, o_ref, kmean_sc, *, nb):
    h = pl.program_id(1)
    i = pl.program_id(2)
    blk = MOBA_BLOCK

    @pl.when(i == 0)
    def _():
        kmean_sc[...] = jnp.zeros_like(kmean_sc)
        for n in range(nb):
            kb = k_ref[0, n * blk:(n + 1) * blk, :].astype(F32)
            kmean_sc[n:n + 1, :] = jnp.mean(kb, axis=0, keepdims=True)

    slope = slope_ref[h]
    q = q_ref[0]
    gate = lax.dot_general(q.astype(F32), kmean_sc[...], _NT, preferred_element_type=F32,
                           precision=lax.Precision.HIGHEST)
    n_iota = lax.broadcasted_iota(I32, gate.shape, 1)
    rank = jnp.zeros(gate.shape, I32)
    for m in range(nb):
        gm = gate[:, m:m + 1]
        beats = (gm > gate) | ((gm == gate) & (m < n_iota))
        rank = rank + jnp.where(beats, (m < i).astype(I32), 0)
    sel = ((n_iota < i) & (rank < MOBA_TOPK)).astype(F32)

    row = lax.broadcasted_iota(I32, (blk, blk), 0)
    col = lax.broadcasted_iota(I32, (blk, blk), 1)
    rel = (row - col).astype(F32)

    def logits(n):
        kn = k_ref[0, pl.ds(pl.multiple_of(n * blk, blk), blk), :]
        s = lax.dot_general(q, kn, _NT, preferred_element_type=F32)
        dist = rel + ((i - n) * blk).astype(F32)
        return s - slope * dist

    def pv(p, n):
        vn = v_ref[0, pl.ds(pl.multiple_of(n * blk, blk), blk), :]
        return jnp.dot(p.astype(BF16), vn, preferred_element_type=F32)

    lg = jnp.where(col <= row, logits(i), NEG)
    m0 = jnp.max(lg, axis=-1, keepdims=True)
    p = jnp.exp(lg - m0)
    l0 = jnp.sum(p, axis=-1, keepdims=True)
    acc0 = pv(p, i)

    def body(n, carry):
        m_prev, l_prev, acc_prev = carry
        keep = jnp.sum(jnp.where(n_iota == n, sel, 0.0), axis=-1, keepdims=True) > 0.5
        lg = jnp.where(keep, logits(n), NEG)
        m_new = jnp.maximum(m_prev, jnp.max(lg, axis=-1, keepdims=True))
        alpha = jnp.exp(m_prev - m_new)
        p = jnp.exp(lg - m_new)
        l_new = alpha * l_prev + jnp.sum(p, axis=-1, keepdims=True)
        acc_new = alpha * acc_prev + pv(p, n)
        return m_new, l_new, acc_new

    _, l_fin, acc_fin = lax.fori_loop(0, i, body, (m0, l0, acc0))
    o_ref[0] = (acc_fin / l_fin).astype(o_ref.dtype)


def _moba(slopes, qk, vq, q_col0, k_col0, v_col0):
    b, s, _ = qk.shape
    assert s % MOBA_BLOCK == 0
    nb = s // MOBA_BLOCK
    assert nb <= LANES
    return pl.pallas_call(
        functools.partial(_moba_kernel, nb=nb),
        grid=(b, MOBA_HEADS, nb),
        in_specs=[pl.BlockSpec(memory_space=pltpu.SMEM),
                  pl.BlockSpec((1, MOBA_BLOCK, HEAD_DIM), lambda bi, h, i: (bi, i, q_col0 + h)),
                  pl.BlockSpec((1, s, HEAD_DIM), lambda bi, h, i: (bi, 0, k_col0 + h)),
                  pl.BlockSpec((1, s, HEAD_DIM), lambda bi, h, i: (bi, 0, v_col0 + h))],
        out_specs=pl.BlockSpec((1, MOBA_BLOCK, HEAD_DIM), lambda bi, h, i: (bi, i, h)),
        out_shape=jax.ShapeDtypeStruct((b, s, MOBA_HEADS * HEAD_DIM), BF16),
        scratch_shapes=[pltpu.VMEM((LANES, HEAD_DIM), F32)],
        compiler_params=_cparams(("arbitrary", "arbitrary", "arbitrary")),
        name="moba_attention",
    )(slopes, qk, qk, vq)


def _dsa_select_kernel(qi_ref, ki_ref, wi_ref, bias_ref, key_sc, *, tq, tk, nsel, hgroup):
    j = pl.program_id(1)
    t0 = j * tq
    nkc = (t0 + tq + tk - 1) // tk
    n_hg = IDX_HEADS // hgroup

    bias_ref[...] = jnp.full(bias_ref.shape, NEG, bias_ref.dtype)
    w = wi_ref[0]
    row_t = t0 + lax.broadcasted_iota(I32, (tq, tk), 0)
    col_i = lax.broadcasted_iota(I32, (tq, tk), 1)

    def score_chunk(c, _):
        c0 = pl.multiple_of(c * tk, tk)
        kc = ki_ref[0, pl.ds(c0, tk), :]
        acc = jnp.zeros((tq, tk), F32)
        for g in range(n_hg):
            qg = qi_ref[0, g * hgroup:(g + 1) * hgroup].reshape(hgroup * tq, IDX_DIM)
            d = lax.dot_general(qg, kc, _NT, preferred_element_type=F32)
            for hh in range(hgroup):
                hd = g * hgroup + hh
                acc = acc + w[:, hd:hd + 1] * jnp.maximum(d[hh * tq:(hh + 1) * tq], 0.0)
        bits = lax.bitcast_convert_type(acc, I32)
        key = jnp.where(bits >= 0, bits, bits ^ 0x7FFFFFFF)
        key = jnp.where(c0 + col_i <= row_t, key, INT_MIN)
        key_sc[:, pl.ds(c0, tk)] = key
        return 0

    lax.fori_loop(0, nkc, score_chunk, 0)

    def bit_body(bi, prefix):
        cand = prefix + jnp.left_shift(jnp.int32(1), 31 - bi)

        def cnt_chunk(c, cnt):
            blk = key_sc[:, pl.ds(pl.multiple_of(c * tk, tk), tk)]
            ge = jnp.where(blk >= cand, 1.0, 0.0)
            for g in range(tk // LANES):
                cnt = cnt + ge[:, g * LANES:(g + 1) * LANES]
            return cnt

        cnt = lax.fori_loop(0, nkc, cnt_chunk, jnp.zeros((tq, LANES), F32))
        total = jnp.sum(cnt, axis=-1, keepdims=True)
        return jnp.where(total >= nsel, cand, prefix)

    thr = lax.fori_loop(0, 32, bit_body, jnp.full((tq, 1), INT_MIN, I32))

    def out_chunk(c, _):
        c0 = pl.multiple_of(c * tk, tk)
        blk = key_sc[:, pl.ds(c0, tk)]
        keep = (blk >= thr) & (c0 + col_i <= row_t)
        bias_ref[0, :, pl.ds(c0, tk)] = jnp.where(keep, 0.0, NEG).astype(bias_ref.dtype)
        return 0

    lax.fori_loop(0, nkc, out_chunk, 0)


def _dsa_select(qi_t, ki, wi):
    b, _, s, _ = qi_t.shape
    tq = _tile(s, 128)
    tk = _tile(s, 512)
    nsel = min(DSA_TOPK_MAX, s // 4)
    return pl.pallas_call(
        functools.partial(_dsa_select_kernel, tq=tq, tk=tk, nsel=nsel, hgroup=8),
        grid=(b, s // tq),
        in_specs=[pl.BlockSpec((1, IDX_HEADS, tq, IDX_DIM), lambda bi, j: (bi, 0, j, 0)),
                  pl.BlockSpec((1, s, IDX_DIM), lambda bi, j: (bi, 0, 0)),
                  pl.BlockSpec((1, tq, IDX_HEADS), lambda bi, j: (bi, j, 0))],
        out_specs=pl.BlockSpec((1, tq, s), lambda bi, j: (bi, j, 0)),
        out_shape=jax.ShapeDtypeStruct((b, s, s), BF16),
        scratch_shapes=[pltpu.VMEM((tq, s), I32)],
        compiler_params=_cparams(("arbitrary", "arbitrary")),
        name="dsa_select",
    )(qi_t, ki, wi)


def _dsa_attn_kernel(slope_ref, q_ref, k_ref, v_ref, bias_ref, o_ref, *, tq, tk, grp):
    g = pl.program_id(1)
    j = pl.program_id(2)
    t0 = j * tq
    nk = (t0 + tq + tk - 1) // tk
    rows = grp * tq
    q = jnp.concatenate([q_ref[0, :, r * HEAD_DIM:(r + 1) * HEAD_DIM] for r in range(grp)], axis=0)
    row_t = t0 + lax.broadcasted_iota(I32, (tq, tk), 0)
    col_i = lax.broadcasted_iota(I32, (tq, tk), 1)

    def body(c, carry):
        m_prev, l_prev, acc_prev = carry
        c0 = pl.multiple_of(c * tk, tk)
        kc = k_ref[0, pl.ds(c0, tk), :]
        vc = v_ref[0, pl.ds(c0, tk), :]
        s = lax.dot_general(q, kc, _NT, preferred_element_type=F32)
        dist = (row_t - (c0 + col_i)).astype(F32)
        mb = bias_ref[0, :, pl.ds(c0, tk)].astype(F32)
        lg = jnp.concatenate(
            [s[r * tq:(r + 1) * tq] + (mb - slope_ref[g * grp + r] * dist) for r in range(grp)], axis=0)
        m_new = jnp.maximum(m_prev, jnp.max(lg, axis=-1, keepdims=True))
        alpha = jnp.exp(m_prev - m_new)
        p = jnp.exp(lg - m_new)
        l_new = alpha * l_prev + jnp.sum(p, axis=-1, keepdims=True)
        acc_new = alpha * acc_prev + jnp.dot(p.astype(BF16), vc, preferred_element_type=F32)
        return m_new, l_new, acc_new

    init = (jnp.full((rows, 1), NEG, F32), jnp.zeros((rows, 1), F32), jnp.zeros((rows, HEAD_DIM), F32))
    _, l_fin, acc_fin = lax.fori_loop(0, nk, body, init)
    out = acc_fin / l_fin
    for r in range(grp):
        o_ref[0, :, r * HEAD_DIM:(r + 1) * HEAD_DIM] = out[r * tq:(r + 1) * tq].astype(o_ref.dtype)


def _dsa_attn(slopes, qk, vq, bias, q_col0, k_col0, v_col0):
    b, s, _ = qk.shape
    grp = DSA_HEADS // DSA_KV_HEADS
    gw = grp * HEAD_DIM
    assert (q_col0 * HEAD_DIM) % gw == 0
    q_blk0 = q_col0 * HEAD_DIM // gw
    tq = _tile(s, 128)
    tk = _tile(s, 512)
    return pl.pallas_call(
        functools.partial(_dsa_attn_kernel, tq=tq, tk=tk, grp=grp),
        grid=(b, DSA_KV_HEADS, s // tq),
        in_specs=[pl.BlockSpec(memory_space=pltpu.SMEM),
                  pl.BlockSpec((1, tq, gw), lambda bi, g, j: (bi, j, q_blk0 + g)),
                  pl.BlockSpec((1, s, HEAD_DIM), lambda bi, g, j: (bi, 0, k_col0 + g)),
                  pl.BlockSpec((1, s, HEAD_DIM), lambda bi, g, j: (bi, 0, v_col0 + g)),
                  pl.BlockSpec((1, tq, s), lambda bi, g, j: (bi, j, 0))],
        out_specs=pl.BlockSpec((1, tq, gw), lambda bi, g, j: (bi, j, g)),
        out_shape=jax.ShapeDtypeStruct((b, s, DSA_HEADS * HEAD_DIM), BF16),
        compiler_params=_cparams(("arbitrary", "arbitrary", "arbitrary")),
        name="dsa_attention",
    )(slopes, qk, qk, vq, bias)


def _merge_kernel(oa_ref, wa_ref, ob_ref, wb_ref, ga_ref, gb_ref, o_ref):
    ya = jnp.dot(oa_ref[0], wa_ref[...], preferred_element_type=F32)
    yb = jnp.dot(ob_ref[0], wb_ref[...], preferred_element_type=F32)
    o_ref[0] = (ga_ref[0].astype(F32) * ya + gb_ref[0].astype(F32) * yb).astype(o_ref.dtype)


def _merge(oa, wa, ob, wb, gates):
    b, s, ka = oa.shape
    kb = ob.shape[2]
    n = wa.shape[1]
    tm, tn = _tile(s, 1024), _tile(n, 512)
    nj = n // tn
    return pl.pallas_call(
        _merge_kernel,
        grid=(b, s // tm, nj),
        in_specs=[pl.BlockSpec((1, tm, ka), lambda bi, i, j: (bi, i, 0)),
                  pl.BlockSpec((ka, tn), lambda bi, i, j: (0, j)),
                  pl.BlockSpec((1, tm, kb), lambda bi, i, j: (bi, i, 0)),
                  pl.BlockSpec((kb, tn), lambda bi, i, j: (0, j)),
                  pl.BlockSpec((1, tm, tn), lambda bi, i, j: (bi, i, j)),
                  pl.BlockSpec((1, tm, tn), lambda bi, i, j: (bi, i, nj + j))],
        out_specs=pl.BlockSpec((1, tm, tn), lambda bi, i, j: (bi, i, j)),
        out_shape=jax.ShapeDtypeStruct((b, s, n), BF16),
        compiler_params=_cparams(("arbitrary", "arbitrary", "arbitrary")),
        name="gated_merge",
    )(oa, wa, ob, wb, gates, gates)


def _resid_kernel(a_ref, w_ref, x_ref, g_ref, o_ref):
    y = jnp.dot(a_ref[0], w_ref[...], preferred_element_type=F32)
    o_ref[0] = x_ref[0] + g_ref[0] * y


def _resid_mm(a, w, x, g):
    b, s, k = a.shape
    n = w.shape[1]
    tm, tn = _tile(s, 1024), _tile(n, 512)
    return pl.pallas_call(
        _resid_kernel,
        grid=(b, s // tm, n // tn),
        in_specs=[pl.BlockSpec((1, tm, k), lambda bi, i, j: (bi, i, 0)),
                  pl.BlockSpec((k, tn), lambda bi, i, j: (0, j)),
                  pl.BlockSpec((1, tm, tn), lambda bi, i, j: (bi, i, j)),
                  pl.BlockSpec((1, 1, tn), lambda bi, i, j: (bi, 0, j))],
        out_specs=pl.BlockSpec((1, tm, tn), lambda bi, i, j: (bi, i, j)),
        out_shape=jax.ShapeDtypeStruct((b, s, n), F32),
        compiler_params=_cparams(("arbitrary", "arbitrary", "arbitrary")),
        name="out_proj_residual",
    )(a, w, x, g.reshape(b, 1, n))


def _swiglu_kernel(a_ref, w1_ref, w3_ref, o_ref):
    a = a_ref[0]
    u = jnp.dot(a, w1_ref[...], preferred_element_type=F32)
    v = jnp.dot(a, w3_ref[...], preferred_element_type=F32)
    o_ref[0] = (u * jax.nn.sigmoid(u) * v).astype(o_ref.dtype)


def _swiglu_up(a, w1, w3):
    b, s, k = a.shape
    n = w1.shape[1]
    tm, tn = _tile(s, 1024), _tile(n, 512)
    return pl.pallas_call(
        _swiglu_kernel,
        grid=(b, s // tm, n // tn),
        in_specs=[pl.BlockSpec((1, tm, k), lambda bi, i, j: (bi, i, 0)),
                  pl.BlockSpec((k, tn), lambda bi, i, j: (0, j)),
                  pl.BlockSpec((k, tn), lambda bi, i, j: (0, j))],
        out_specs=pl.BlockSpec((1, tm, tn), lambda bi, i, j: (bi, i, j)),
        out_shape=jax.ShapeDtypeStruct((b, s, n), BF16),
        compiler_params=_cparams(("arbitrary", "arbitrary", "arbitrary")),
        name="shared_swiglu_up",
    )(a, w1, w3)


def _final_kernel(a_ref, w_ref, *rest):
    yk_refs, (x_ref, g_ref, o_ref) = rest[:EXPERT_TOPK], rest[EXPERT_TOPK:]
    y = jnp.dot(a_ref[0], w_ref[...], preferred_element_type=F32)
    for r in yk_refs:
        y = y + r[...]
    o_ref[0] = x_ref[0] + g_ref[0] * y


def _final(hs, w2, yk, x, g):
    b, s, k = hs.shape
    n = w2.shape[1]
    tm, tn = _tile(s, 512), _tile(n, 512)
    nti = s // tm
    t_blocks = b * nti
    yk_specs = [pl.BlockSpec((tm, tn), lambda bi, i, j, kk=kk: (kk * t_blocks + bi * nti + i, j))
                for kk in range(EXPERT_TOPK)]
    return pl.pallas_call(
        _final_kernel,
        grid=(b, nti, n // tn),
        in_specs=[pl.BlockSpec((1, tm, k), lambda bi, i, j: (bi, i, 0)),
                  pl.BlockSpec((k, tn), lambda bi, i, j: (0, j))] + yk_specs + [
                  pl.BlockSpec((1, tm, tn), lambda bi, i, j: (bi, i, j)),
                  pl.BlockSpec((1, 1, tn), lambda bi, i, j: (bi, 0, j))],
        out_specs=pl.BlockSpec((1, tm, tn), lambda bi, i, j: (bi, i, j)),
        out_shape=jax.ShapeDtypeStruct((b, s, n), F32),
        compiler_params=_cparams(("arbitrary", "arbitrary", "arbitrary")),
        name="moe_combine_residual",
    )(hs, w2, *([yk] * EXPERT_TOPK), x, g.reshape(b, 1, n))


def _router_kernel(h_ref, wr_ref, b_ref, idx_ref, wgt_ref):
    tm = h_ref.shape[0]
    per = N_EXPERTS // N_GROUPS
    logits = lax.dot_general(wr_ref[...], h_ref[...], _NT, preferred_element_type=F32)
    scores = jax.nn.sigmoid(logits)
    choice = scores + b_ref[...]
    sub = lax.broadcasted_iota(I32, (per, tm), 0).astype(F32)
    gs = []
    for g in range(N_GROUPS):
        cg = choice[g * per:(g + 1) * per]
        m1 = jnp.max(cg, axis=0, keepdims=True)
        i1 = jnp.min(jnp.where(cg == m1, sub, float(per)), axis=0, keepdims=True)
        m2 = jnp.max(jnp.where(sub == i1, -jnp.inf, cg), axis=0, keepdims=True)
        gs.append(m1 + m2)
    masked = []
    for g in range(N_GROUPS):
        rank = jnp.zeros((1, tm), F32)
        for g2 in range(N_GROUPS):
            if g2 == g:
                continue
            beats = (gs[g2] > gs[g]) | ((gs[g2] == gs[g]) & (g2 < g))
            rank = rank + jnp.where(beats, 1.0, 0.0)
        masked.append(jnp.where(rank < TOPK_GROUPS, choice[g * per:(g + 1) * per], -jnp.inf))
    cm = jnp.concatenate(masked, axis=0)
    erow = lax.broadcasted_iota(I32, (N_EXPERTS, tm), 0).astype(F32)
    idxs, wts = [], []
    for _ in range(EXPERT_TOPK):
        mx = jnp.max(cm, axis=0, keepdims=True)
        ix = jnp.min(jnp.where(cm == mx, erow, float(N_EXPERTS)), axis=0, keepdims=True)
        hit = erow == ix
        idxs.append(ix)
        wts.append(jnp.sum(jnp.where(hit, scores, 0.0), axis=0, keepdims=True))
        cm = jnp.where(hit, -jnp.inf, cm)
    wt = jnp.concatenate(wts, axis=0)
    idx_ref[...] = jnp.concatenate(idxs, axis=0).astype(I32)
    wgt_ref[...] = wt / jnp.sum(wt, axis=0, keepdims=True) * ROUTED_SCALE


def _router(h2_flat, wr_t, bias_col):
    t, d = h2_flat.shape
    tm = _tile(t, 512)
    return pl.pallas_call(
        _router_kernel,
        grid=(t // tm,),
        in_specs=[pl.BlockSpec((tm, d), lambda i: (i, 0)),
                  pl.BlockSpec((N_EXPERTS, d), lambda i: (0, 0)),
                  pl.BlockSpec((N_EXPERTS, 1), lambda i: (0, 0))],
        out_specs=[pl.BlockSpec((EXPERT_TOPK, tm), lambda i: (0, i)),
                   pl.BlockSpec((EXPERT_TOPK, tm), lambda i: (0, i))],
        out_shape=[jax.ShapeDtypeStruct((EXPERT_TOPK, t), I32),
                   jax.ShapeDtypeStruct((EXPERT_TOPK, t), F32)],
        compiler_params=_cparams(("arbitrary",)),
        name="router_topk",
    )(h2_flat, wr_t, bias_col)


def _moe_kernel(ech_ref, nact_ref, nval_ref, tokc_ref, tokn_ref, dst_ref, wt_ref, h_hbm, w1_ref, w3_ref,
                w2_ref, yk_hbm, xs, ys, gsem, ssem):
    del ech_ref
    c = pl.program_id(0)
    nc = pl.num_programs(0)
    nact = nact_ref[0]
    rows = MOE_CHUNK
    slot = c % 2

    def gather_copy(tok_ref, r, s):
        return pltpu.make_async_copy(h_hbm.at[pl.ds(tok_ref[0, 0, r], 1), :],
                                     xs.at[s, pl.ds(r, 1), :], gsem.at[s])

    def scatter_copy(r, dst_row):
        return pltpu.make_async_copy(ys.at[pl.ds(r, 1), :], yk_hbm.at[pl.ds(dst_row, 1), :], ssem.at[0])

    def start_gather(tok_ref, s):
        def body(r, carry):
            gather_copy(tok_ref, r, s).start()
            return carry
        lax.fori_loop(0, rows, body, 0)

    def wait_scatter(chunk):
        def body(r, carry):
            scatter_copy(r, 0).wait()
            return carry
        lax.fori_loop(0, nval_ref[chunk], body, 0)

    @pl.when(c == 0)
    def _():
        start_gather(tokc_ref, 0)

    @pl.when(c + 1 < nact)
    def _():
        start_gather(tokn_ref, 1 - slot)

    @pl.when(c < nact)
    def _():
        def wbody(r, carry):
            gather_copy(tokc_ref, r, slot).wait()
            return carry
        lax.fori_loop(0, rows, wbody, 0)

        x = xs[slot].astype(BF16)
        u = jnp.dot(x, w1_ref[0], preferred_element_type=F32)
        v = jnp.dot(x, w3_ref[0], preferred_element_type=F32)
        hm = (u * jax.nn.sigmoid(u) * v).astype(BF16)
        y = jnp.dot(hm, w2_ref[0], preferred_element_type=F32) * wt_ref[0]

        @pl.when(c > 0)
        def _():
            wait_scatter(c - 1)

        ys[...] = y

        def sbody(r, carry):
            scatter_copy(r, dst_ref[0, 0, r]).start()
            return carry
        lax.fori_loop(0, nval_ref[c], sbody, 0)

    @pl.when(c == nc - 1)
    def _():
        wait_scatter(nact - 1)


def _moe(e_chunk, n_active, n_valid, tok_pad, dst_pad, w_pad, h2_f32, w1, w3, w2, yk_rows):
    t, d = h2_f32.shape
    n_chunks = tok_pad.shape[0]
    f = w1.shape[2]
    rows = MOE_CHUNK
    smem_blk = lambda fn: pl.BlockSpec((1, 1, rows), fn, memory_space=pltpu.SMEM)
    grid_spec = pltpu.PrefetchScalarGridSpec(
        num_scalar_prefetch=3,
        grid=(n_chunks,),
        in_specs=[smem_blk(lambda c, ech, na, nv: (c, 0, 0)),
                  smem_blk(lambda c, ech, na, nv: (jnp.minimum(c + 1, n_chunks - 1), 0, 0)),
                  smem_blk(lambda c, ech, na, nv: (c, 0, 0)),
                  pl.BlockSpec((1, rows, 1), lambda c, ech, na, nv: (c, 0, 0)),
                  pl.BlockSpec(memory_space=pl.ANY),
                  pl.BlockSpec((1, d, f), lambda c, ech, na, nv: (ech[c], 0, 0)),
                  pl.BlockSpec((1, d, f), lambda c, ech, na, nv: (ech[c], 0, 0)),
                  pl.BlockSpec((1, f, d), lambda c, ech, na, nv: (ech[c], 0, 0))],
        out_specs=pl.BlockSpec(memory_space=pl.ANY),
        scratch_shapes=[pltpu.VMEM((2, rows, d), F32),
                        pltpu.VMEM((rows, d), F32),
                        pltpu.SemaphoreType.DMA((2,)),
                        pltpu.SemaphoreType.DMA((1,))],
    )
    return pl.pallas_call(
        _moe_kernel,
        grid_spec=grid_spec,
        out_shape=jax.ShapeDtypeStruct((yk_rows, d), F32),
        compiler_params=_cparams(("arbitrary",)),
        name="routed_experts",
    )(e_chunk, n_active, n_valid, tok_pad, tok_pad, dst_pad, w_pad, h2_f32, w1, w3, w2)


def _dispatch(idx_t, wgt_t):
    k, t = idx_t.shape
    e, c = N_EXPERTS, MOE_CHUNK
    m = k * t
    p = m + e * c
    n_chunks = p // c
    e_flat = idx_t.reshape(m)
    w_flat = wgt_t.reshape(m)
    order = jnp.argsort(e_flat).astype(I32)
    counts = jnp.sum((e_flat[None, :] == jnp.arange(e, dtype=I32)[:, None]).astype(I32), axis=1)
    starts = jnp.cumsum(counts) - counts
    padded = (counts + c - 1) // c * c
    pends = jnp.cumsum(padded)
    pstarts = pends - padded
    pos = jnp.arange(p, dtype=I32)
    e_of_p = jnp.minimum(jnp.sum((pos[:, None] >= pends[None, :]).astype(I32), axis=1), e - 1)
    off = pos - pstarts[e_of_p]
    valid = (off < counts[e_of_p]) & (pos < pends[-1])
    src = jnp.where(valid, starts[e_of_p] + off, 0)
    a_p = order[src]
    tok_pad = jnp.where(valid, a_p % t, 0).astype(I32)
    w_pad = jnp.where(valid, w_flat[a_p], 0.0).astype(F32)
    dst_pad = jnp.where(valid, a_p, 0).astype(I32)
    e_chunk = e_of_p[::c].astype(I32)
    n_active = (pends[-1] // c).astype(I32).reshape(1)
    n_valid = jnp.sum(valid.reshape(n_chunks, c).astype(I32), axis=1)
    return (e_chunk, n_active, n_valid, tok_pad.reshape(n_chunks, 1, c), dst_pad.reshape(n_chunks, 1, c),
            w_pad.reshape(n_chunks, c, 1), m)


def _alibi_slopes(n_heads):
    return 2.0 ** (-8.0 * jnp.arange(1, n_heads + 1, dtype=F32) / n_heads)


def _layer(x, c_pad, ln1_gain, ln2_gain, w_ada, b_ada, w_in, moba_q_gain, moba_k_gain, dsa_q_gain,
           dsa_k_gain, idx_k_gain, w_proj_moba, w_proj_dsa, w_out, w_router, router_bias,
           w1_experts, w3_experts, w2_experts, w1_shared, w3_shared, w2_shared):
    b, s, d = x.shape
    t = b * s
    mw = MOBA_HEADS * HEAD_DIM
    qw = DSA_HEADS * HEAD_DIM
    kvw = DSA_KV_HEADS * HEAD_DIM
    iw = IDX_HEADS * IDX_DIM

    mod = _ada(c_pad, w_ada, b_ada.reshape(1, -1))[:b]
    sh1, sc1, g1, sh2, sc2, g2 = jnp.split(mod, 6, axis=-1)

    (h,) = _normmod(x, ln1_gain, sc1, sh1, with_f32=False)

    o_qa, o_ka, o_va = 0, mw, 2 * mw
    o_qb = 3 * mw
    o_kb = o_qb + qw
    o_vb = o_kb + kvw
    o_qi = o_vb + kvw
    o_ki = o_qi + iw
    o_wi = o_ki + IDX_DIM
    o_ga = o_wi + IDX_HEADS
    o_gb = o_ga + d
    cols = lambda a, n: w_in[:, a:a + n]

    scale = HEAD_DIM ** -0.5
    w_qk = jnp.concatenate([cols(o_qa, mw), cols(o_ka, mw), cols(o_qb, qw), cols(o_kb, kvw)], axis=1).astype(BF16)
    g_qk = jnp.concatenate([jnp.tile(moba_q_gain * scale, MOBA_HEADS), jnp.tile(moba_k_gain, MOBA_HEADS),
                            jnp.tile(dsa_q_gain * scale, DSA_HEADS), jnp.tile(dsa_k_gain, DSA_KV_HEADS)])
    qk = _proj(h, w_qk, g_qk.reshape(1, -1), "headnorm", BF16)
    w_vq = jnp.concatenate([cols(o_va, mw), cols(o_vb, kvw), cols(o_qi, iw)], axis=1).astype(BF16)
    vq = _proj(h, w_vq, jnp.zeros((1, w_vq.shape[1]), F32), "plain", BF16)
    w_gate = cols(o_ga, 2 * d).astype(BF16)
    gates = _proj(h, w_gate, jnp.zeros((1, 2 * d), F32), "sigmoid", BF16)
    w_ix = jnp.pad(cols(o_ki, IDX_DIM + IDX_HEADS), ((0, 0), (0, LANES - IDX_DIM - IDX_HEADS))).astype(BF16)
    aux_ix = jnp.concatenate([idx_k_gain, jnp.full((IDX_HEADS,), IDX_HEADS ** -0.5 * IDX_DIM ** -0.5, F32),
                              jnp.zeros((LANES - IDX_DIM - IDX_HEADS,), F32)])
    kiw = _proj(h, w_ix, aux_ix.reshape(1, -1), "indexer", F32, tn_pref=LANES)

    hd = HEAD_DIM
    o_a = _moba(_alibi_slopes(MOBA_HEADS), qk, vq, q_col0=0, k_col0=mw // hd, v_col0=0)

    qi_t = vq[:, :, mw + kvw:].reshape(b, s, IDX_HEADS, IDX_DIM).transpose(0, 2, 1, 3)
    ki = kiw[:, :, :IDX_DIM].astype(BF16)
    wi = kiw[:, :, IDX_DIM:IDX_DIM + IDX_HEADS]
    bias = _dsa_select(qi_t, ki, wi)
    o_b = _dsa_attn(_alibi_slopes(DSA_HEADS), qk, vq, bias, q_col0=2 * mw // hd,
                    k_col0=(2 * mw + qw) // hd, v_col0=mw // hd)

    merged = _merge(o_a, w_proj_moba.astype(BF16), o_b, w_proj_dsa.astype(BF16), gates)
    x1 = _resid_mm(merged, w_out.astype(BF16), x, g1)

    h2, h2_f32 = _normmod(x1, ln2_gain, sc2, sh2, with_f32=True)
    wr_t = w_router.T.astype(BF16)
    idx_t, wgt_t = _router(h2.reshape(t, d), wr_t, router_bias.reshape(-1, 1))
    e_chunk, n_active, n_valid, tok_pad, dst_pad, w_pad, p_rows = _dispatch(idx_t, wgt_t)
    yk = _moe(e_chunk, n_active, n_valid, tok_pad, dst_pad, w_pad, h2_f32.reshape(t, d),
              w1_experts.astype(BF16), w3_experts.astype(BF16), w2_experts.astype(BF16), p_rows)
    hs = _swiglu_up(h2, w1_shared.astype(BF16), w3_shared.astype(BF16))
    return _final(hs, w2_shared.astype(BF16), yk, x1, g2)


def kernel(x, c, ln1_gain, ln2_gain, w_ada, b_ada, w_in, moba_q_gain, moba_k_gain, dsa_q_gain, dsa_k_gain, idx_k_gain, w_proj_moba, w_proj_dsa, w_out, w_router, router_bias, w1_experts, w3_experts, w2_experts, w1_shared, w3_shared, w2_shared):
    depth = w_ada.shape[0]
    b = x.shape[0]
    c_pad = jnp.pad(c, ((0, (-b) % 8), (0, 0)))
    params = (ln1_gain, ln2_gain, w_ada, b_ada, w_in, moba_q_gain, moba_k_gain, dsa_q_gain, dsa_k_gain,
              idx_k_gain, w_proj_moba, w_proj_dsa, w_out, w_router, router_bias, w1_experts, w3_experts,
              w2_experts, w1_shared, w3_shared, w2_shared)
    for l in range(depth):
        x = _layer(x, c_pad, *[p[l] for p in params])
    return x
```

```python
import functools

import jax
import jax.numpy as jnp
from jax import lax
from jax.experimental import pallas as pl
from jax.experimental.pallas import tpu as pltpu

F32 = jnp.float32
BF16 = jnp.bfloat16
I32 = jnp.int32
U32 = jnp.uint32

HEAD_DIM = 128
MOBA_HEADS = 16
MOBA_BLOCK = 256
MOBA_TOPK = 3
DSA_HEADS = 16
DSA_KV_HEADS = 4
DSA_TOPK_MAX = 256
IDX_HEADS = 32
IDX_DIM = 64
N_EXPERTS = 64
EXPERT_TOPK = 8
N_GROUPS = 8
TOPK_GROUPS = 4
ROUTED_SCALE = 2.5
NORM_EPS = 1e-6

LANES = 128
V7X_VMEM_LIMIT_BYTES = 56 * 1024 * 1024

MOE_CHUNK = 256
YK_PACK_TILE = 512
NEG = -1e30
INT_MIN = -2 ** 31

_NT = (((1,), (1,)), ((), ()))


def _cparams(sem):
    return pltpu.CompilerParams(dimension_semantics=sem, vmem_limit_bytes=V7X_VMEM_LIMIT_BYTES)


def _tile(n, pref):
    if n <= pref:
        return n
    t = pref - pref % LANES
    while t > LANES and n % t:
        t -= LANES
    assert n % t == 0, (n, pref)
    return t


def _ada_kernel(c_ref, w_ref, b_ref, o_ref):
    c = c_ref[...]
    s = c * jax.nn.sigmoid(c)
    acc = jnp.dot(s.astype(BF16), w_ref[...].astype(BF16), preferred_element_type=F32)
    o_ref[...] = acc + b_ref[...]


def _ada(c_pad, w_ada, b_ada):
    rows, d = c_pad.shape
    n = w_ada.shape[1]
    tn = _tile(n, 512)
    return pl.pallas_call(
        _ada_kernel,
        grid=(n // tn,),
        in_specs=[pl.BlockSpec((rows, d), lambda j: (0, 0)),
                  pl.BlockSpec((d, tn), lambda j: (0, j)),
                  pl.BlockSpec((1, tn), lambda j: (0, j))],
        out_specs=pl.BlockSpec((rows, tn), lambda j: (0, j)),
        out_shape=jax.ShapeDtypeStruct((rows, n), F32),
        compiler_params=_cparams(("arbitrary",)),
        name="ada_mod",
    )(c_pad, w_ada, b_ada)


def _pack_bf16_pairs(v):
    n = v.shape[1] // 2
    bits = lax.bitcast_convert_type(v.astype(F32), U32)
    return (bits[:, :n] >> 16) | bits[:, n:]


def _unpack_bf16_pairs(w):
    lo = lax.bitcast_convert_type(w << 16, F32)
    hi = lax.bitcast_convert_type(w & jnp.uint32(0xFFFF0000), F32)
    return lo, hi


def _normmod_kernel(x_ref, gain_ref, sc_ref, sh_ref, *o_refs):
    x = x_ref[0]
    y = x * lax.rsqrt(jnp.mean(x * x, axis=-1, keepdims=True) + NORM_EPS) * gain_ref[...]
    h = (y * (1.0 + sc_ref[0]) + sh_ref[0]).astype(BF16)
    o_refs[0][0] = h
    if len(o_refs) > 1:
        o_refs[1][0] = _pack_bf16_pairs(h)


def _normmod(x, gain, sc, sh, with_packed):
    b, s, d = x.shape
    ts = _tile(s, 256)
    blk = pl.BlockSpec((1, ts, d), lambda bi, i: (bi, i, 0))
    vec = pl.BlockSpec((1, 1, d), lambda bi, i: (bi, 0, 0))
    out_shape = [jax.ShapeDtypeStruct((b, s, d), BF16)]
    out_specs = [blk]
    if with_packed:
        out_shape.append(jax.ShapeDtypeStruct((b, s, d // 2), U32))
        out_specs.append(pl.BlockSpec((1, ts, d // 2), lambda bi, i: (bi, i, 0)))
    return pl.pallas_call(
        _normmod_kernel,
        grid=(b, s // ts),
        in_specs=[blk, pl.BlockSpec((1, d), lambda bi, i: (0, 0)), vec, vec],
        out_specs=out_specs,
        out_shape=out_shape,
        compiler_params=_cparams(("arbitrary", "arbitrary")),
        name="norm_modulate",
    )(x, gain.reshape(1, d), sc.reshape(b, 1, d), sh.reshape(b, 1, d))


def _proj_kernel(a_ref, w_ref, aux_ref, o_ref, *, mode):
    acc = jnp.dot(a_ref[0], w_ref[...], preferred_element_type=F32)
    tn = acc.shape[1]
    if mode == "plain":
        o_ref[0] = acc.astype(o_ref.dtype)
    elif mode == "sigmoid":
        o_ref[0] = jax.nn.sigmoid(acc).astype(o_ref.dtype)
    elif mode == "headnorm":
        for g in range(tn // HEAD_DIM):
            sl = slice(g * HEAD_DIM, (g + 1) * HEAD_DIM)
            blk = acc[:, sl]
            ms = jnp.mean(blk * blk, axis=-1, keepdims=True)
            o_ref[0, :, sl] = (blk * lax.rsqrt(ms + NORM_EPS) * aux_ref[:, sl]).astype(o_ref.dtype)
    elif mode == "indexer":
        lane = lax.broadcasted_iota(I32, acc.shape, 1)
        is_key = lane < IDX_DIM
        ms = jnp.sum(jnp.where(is_key, acc * acc, 0.0), axis=-1, keepdims=True) / IDX_DIM
        o_ref[0] = acc * jnp.where(is_key, lax.rsqrt(ms + NORM_EPS), 1.0) * aux_ref[...]
    else:
        raise ValueError(mode)


def _proj(a, w, aux, mode, out_dtype, tm_pref=1024, tn_pref=512):
    b, s, k = a.shape
    n = w.shape[1]
    tm, tn = _tile(s, tm_pref), _tile(n, tn_pref)
    return pl.pallas_call(
        functools.partial(_proj_kernel, mode=mode),
        grid=(b, s // tm, n // tn),
        in_specs=[pl.BlockSpec((1, tm, k), lambda bi, i, j: (bi, i, 0)),
                  pl.BlockSpec((k, tn), lambda bi, i, j: (0, j)),
                  pl.BlockSpec((1, tn), lambda bi, i, j: (0, j))],
        out_specs=pl.BlockSpec((1, tm, tn), lambda bi, i, j: (bi, i, j)),
        out_shape=jax.ShapeDtypeStruct((b, s, n), out_dtype),
        compiler_params=_cparams(("arbitrary", "arbitrary", "arbitrary")),
        name="proj_" + mode,
    )(a, w, aux)


AUG_ALIBI = 0
AUG_SEL = 8


def _moba_kernel(sp_ref, q_ref, k_ref, v_ref, o_ref, kaug_sc, vt_sc, kmean_sc, *, nb, gt):
    h = pl.program_id(1)
    i = pl.program_id(2)
    blk = MOBA_BLOCK
    nbp = kmean_sc.shape[0]
    gblocks = gt // blk

    @pl.when(i == 0)
    def _():
        kmean_sc[...] = jnp.zeros_like(kmean_sc)
        lane = lax.broadcasted_iota(I32, (blk, LANES), 1)
        row = lax.broadcasted_iota(I32, (blk, LANES), 0)
        for n in range(nb):
            rows = slice(n * blk, (n + 1) * blk)
            kb = k_ref[0, rows, :]
            kmean_sc[n:n + 1, :] = jnp.mean(kb.astype(F32), axis=0, keepdims=True)
            pos = n * blk + row
            alibi = jnp.where((lane & 1) == 0, ((pos >> 4) << 4).astype(F32), (pos & 15).astype(F32))
            aug = jnp.where(lane < AUG_ALIBI + 6, alibi, jnp.where(lane == AUG_SEL + n, 1.0, 0.0))
            kaug_sc[rows, :HEAD_DIM] = kb
            kaug_sc[rows, HEAD_DIM:] = aug.astype(BF16)
            vt_sc[:, rows] = v_ref[0, rows, :].astype(F32).T.astype(BF16)

    q = q_ref[0]
    gate_t = lax.dot_general(kmean_sc[...], q.astype(F32), _NT, preferred_element_type=F32,
                             precision=lax.Precision.HIGHEST)
    r_iota = lax.broadcasted_iota(I32, gate_t.shape, 0)
    rank = jnp.zeros(gate_t.shape, I32)
    for m in range(nb):
        gm = gate_t[m:m + 1, :]
        beats = (gm > gate_t) | ((gm == gate_t) & (m < r_iota))
        rank = rank + jnp.where(beats, (m < i).astype(I32), 0)
    keep_t = ((r_iota < i) & (rank < MOBA_TOPK)) | (r_iota == i)
    selb_t = jnp.where(keep_t, 0.0, NEG)

    sub8 = lax.broadcasted_iota(I32, (8, blk), 0)
    alibi_q = jnp.where(sub8 < 2, sp_ref[h, 0], jnp.where(sub8 < 4, sp_ref[h, 1],
                        jnp.where(sub8 < 6, sp_ref[h, 2], 0.0)))
    augq_t = jnp.concatenate([alibi_q, selb_t, jnp.zeros((LANES - 8 - nbp, blk), F32)], axis=0)
    qa = jnp.concatenate([q, augq_t.T.astype(BF16)], axis=1)

    t_pos = i * blk + lax.broadcasted_iota(I32, (gt, blk), 1)
    s_off = lax.broadcasted_iota(I32, (gt, blk), 0)

    def group(gi, carry, causal):
        m_prev, l_prev, acc_prev = carry
        g0 = pl.multiple_of(gi * gt, gt)
        st = lax.dot_general(kaug_sc[pl.ds(g0, gt), :], qa, _NT, preferred_element_type=F32)
        if causal:
            st = jnp.where(g0 + s_off <= t_pos, st, NEG)
        m_new = jnp.maximum(m_prev, jnp.max(st, axis=0, keepdims=True))
        alpha = jnp.exp(m_prev - m_new)
        p = jnp.exp(st - m_new)
        l_new = alpha * l_prev + jnp.sum(p, axis=0, keepdims=True)
        pv = jnp.dot(vt_sc[:, pl.ds(g0, gt)], p.astype(BF16), preferred_element_type=F32)
        return m_new, l_new, alpha * acc_prev + pv

    init = (jnp.full((1, blk), NEG, F32), jnp.zeros((1, blk), F32), jnp.zeros((HEAD_DIM, blk), F32))
    n_full = i // gblocks
    carry = lax.fori_loop(0, n_full, lambda gi, c: group(gi, c, False), init)
    _, l_fin, acc_fin = group(n_full, carry, True)
    o_ref[0] = (acc_fin / l_fin).T.astype(o_ref.dtype)


def _moba(slope_parts, qk, vq, q_col0, k_col0, v_col0):
    b, s, _ = qk.shape
    assert s % MOBA_BLOCK == 0 and s <= 4096
    nb = s // MOBA_BLOCK
    nbp = -(-nb // 8) * 8
    assert AUG_SEL + nbp <= LANES
    gt = min(s, 4 * MOBA_BLOCK)
    assert s % gt == 0
    return pl.pallas_call(
        functools.partial(_moba_kernel, nb=nb, gt=gt),
        grid=(b, MOBA_HEADS, nb),
        in_specs=[pl.BlockSpec(memory_space=pltpu.SMEM),
                  pl.BlockSpec((1, MOBA_BLOCK, HEAD_DIM), lambda bi, h, i: (bi, i, q_col0 + h)),
                  pl.BlockSpec((1, s, HEAD_DIM), lambda bi, h, i: (bi, 0, k_col0 + h)),
                  pl.BlockSpec((1, s, HEAD_DIM), lambda bi, h, i: (bi, 0, v_col0 + h))],
        out_specs=pl.BlockSpec((1, MOBA_BLOCK, HEAD_DIM), lambda bi, h, i: (bi, i, h)),
        out_shape=jax.ShapeDtypeStruct((b, s, MOBA_HEADS * HEAD_DIM), BF16),
        scratch_shapes=[pltpu.VMEM((s, 2 * HEAD_DIM), BF16),
                        pltpu.VMEM((HEAD_DIM, s), BF16),
                        pltpu.VMEM((nbp, HEAD_DIM), F32)],
        compiler_params=_cparams(("arbitrary", "arbitrary", "arbitrary")),
        name="moba_attention",
    )(slope_parts, qk, qk, vq)


def _dsa_select_kernel(qi_ref, ki_ref, wi_ref, bias_ref, key_sc, *, tq, tk, nsel, hgroup):
    j = pl.program_id(1)
    t0 = j * tq
    nkc = (t0 + tq + tk - 1) // tk
    n_hg = IDX_HEADS // hgroup

    bias_ref[...] = jnp.full(bias_ref.shape, NEG, bias_ref.dtype)
    w = wi_ref[0]
    row_t = t0 + lax.broadcasted_iota(I32, (tq, tk), 0)
    col_i = lax.broadcasted_iota(I32, (tq, tk), 1)

    def score_chunk(c, _):
        c0 = pl.multiple_of(c * tk, tk)
        kc = ki_ref[0, pl.ds(c0, tk), :]
        acc = jnp.zeros((tq, tk), F32)
        for g in range(n_hg):
            qg = qi_ref[0, g * hgroup:(g + 1) * hgroup].reshape(hgroup * tq, IDX_DIM)
            d = lax.dot_general(qg, kc, _NT, preferred_element_type=F32)
            for hh in range(hgroup):
                hd = g * hgroup + hh
                acc = acc + w[:, hd:hd + 1] * jnp.maximum(d[hh * tq:(hh + 1) * tq], 0.0)
        bits = lax.bitcast_convert_type(acc, I32)
        key = jnp.where(bits >= 0, bits, bits ^ 0x7FFFFFFF)
        key = jnp.where(c0 + col_i <= row_t, key, INT_MIN)
        key_sc[:, pl.ds(c0, tk)] = key
        return 0

    lax.fori_loop(0, nkc, score_chunk, 0)

    def bit_body(bi, prefix):
        cand = prefix + jnp.left_shift(jnp.int32(1), 31 - bi)

        def cnt_chunk(c, cnt):
            blk = key_sc[:, pl.ds(pl.multiple_of(c * tk, tk), tk)]
            ge = jnp.where(blk >= cand, 1.0, 0.0)
            for g in range(tk // LANES):
                cnt = cnt + ge[:, g * LANES:(g + 1) * LANES]
            return cnt

        cnt = lax.fori_loop(0, nkc, cnt_chunk, jnp.zeros((tq, LANES), F32))
        total = jnp.sum(cnt, axis=-1, keepdims=True)
        return jnp.where(total >= nsel, cand, prefix)

    thr = lax.fori_loop(0, 32, bit_body, jnp.full((tq, 1), INT_MIN, I32))

    def out_chunk(c, _):
        c0 = pl.multiple_of(c * tk, tk)
        blk = key_sc[:, pl.ds(c0, tk)]
        keep = (blk >= thr) & (c0 + col_i <= row_t)
        bias_ref[0, :, pl.ds(c0, tk)] = jnp.where(keep, 0.0, NEG).astype(bias_ref.dtype)
        return 0

    lax.fori_loop(0, nkc, out_chunk, 0)


def _dsa_select(qi_t, ki, wi):
    b, _, s, _ = qi_t.shape
    tq = _tile(s, 128)
    tk = _tile(s, 512)
    nsel = min(DSA_TOPK_MAX, s // 4)
    return pl.pallas_call(
        functools.partial(_dsa_select_kernel, tq=tq, tk=tk, nsel=nsel, hgroup=8),
        grid=(b, s // tq),
        in_specs=[pl.BlockSpec((1, IDX_HEADS, tq, IDX_DIM), lambda bi, j: (bi, 0, j, 0)),
                  pl.BlockSpec((1, s, IDX_DIM), lambda bi, j: (bi, 0, 0)),
                  pl.BlockSpec((1, tq, IDX_HEADS), lambda bi, j: (bi, j, 0))],
        out_specs=pl.BlockSpec((1, tq, s), lambda bi, j: (bi, j, 0)),
        out_shape=jax.ShapeDtypeStruct((b, s, s), BF16),
        scratch_shapes=[pltpu.VMEM((tq, s), I32)],
        compiler_params=_cparams(("arbitrary", "arbitrary")),
        name="dsa_select",
    )(qi_t, ki, wi)


def _dsa_attn_kernel(slope_ref, q_ref, k_ref, v_ref, bias_ref, o_ref, *, tq, tk, grp):
    g = pl.program_id(1)
    j = pl.program_id(2)
    t0 = j * tq
    nk = (t0 + tq + tk - 1) // tk
    rows = grp * tq
    q = jnp.concatenate([q_ref[0, :, r * HEAD_DIM:(r + 1) * HEAD_DIM] for r in range(grp)], axis=0)
    row_t = t0 + lax.broadcasted_iota(I32, (tq, tk), 0)
    col_i = lax.broadcasted_iota(I32, (tq, tk), 1)

    def body(c, carry):
        m_prev, l_prev, acc_prev = carry
        c0 = pl.multiple_of(c * tk, tk)
        kc = k_ref[0, pl.ds(c0, tk), :]
        vc = v_ref[0, pl.ds(c0, tk), :]
        s = lax.dot_general(q, kc, _NT, preferred_element_type=F32)
        dist = (row_t - (c0 + col_i)).astype(F32)
        mb = bias_ref[0, :, pl.ds(c0, tk)].astype(F32)
        lg = jnp.concatenate(
            [s[r * tq:(r + 1) * tq] + (mb - slope_ref[g * grp + r] * dist) for r in range(grp)], axis=0)
        m_new = jnp.maximum(m_prev, jnp.max(lg, axis=-1, keepdims=True))
        alpha = jnp.exp(m_prev - m_new)
        p = jnp.exp(lg - m_new)
        l_new = alpha * l_prev + jnp.sum(p, axis=-1, keepdims=True)
        acc_new = alpha * acc_prev + jnp.dot(p.astype(BF16), vc, preferred_element_type=F32)
        return m_new, l_new, acc_new

    init = (jnp.full((rows, 1), NEG, F32), jnp.zeros((rows, 1), F32), jnp.zeros((rows, HEAD_DIM), F32))
    _, l_fin, acc_fin = lax.fori_loop(0, nk, body, init)
    out = acc_fin / l_fin
    for r in range(grp):
        o_ref[0, :, r * HEAD_DIM:(r + 1) * HEAD_DIM] = out[r * tq:(r + 1) * tq].astype(o_ref.dtype)


def _dsa_attn(slopes, qk, vq, bias, q_col0, k_col0, v_col0):
    b, s, _ = qk.shape
    grp = DSA_HEADS // DSA_KV_HEADS
    gw = grp * HEAD_DIM
    assert (q_col0 * HEAD_DIM) % gw == 0
    q_blk0 = q_col0 * HEAD_DIM // gw
    tq = _tile(s, 128)
    tk = _tile(s, 512)
    return pl.pallas_call(
        functools.partial(_dsa_attn_kernel, tq=tq, tk=tk, grp=grp),
        grid=(b, DSA_KV_HEADS, s // tq),
        in_specs=[pl.BlockSpec(memory_space=pltpu.SMEM),
                  pl.BlockSpec((1, tq, gw), lambda bi, g, j: (bi, j, q_blk0 + g)),
                  pl.BlockSpec((1, s, HEAD_DIM), lambda bi, g, j: (bi, 0, k_col0 + g)),
                  pl.BlockSpec((1, s, HEAD_DIM), lambda bi, g, j: (bi, 0, v_col0 + g)),
                  pl.BlockSpec((1, tq, s), lambda bi, g, j: (bi, j, 0))],
        out_specs=pl.BlockSpec((1, tq, gw), lambda bi, g, j: (bi, j, g)),
        out_shape=jax.ShapeDtypeStruct((b, s, DSA_HEADS * HEAD_DIM), BF16),
        compiler_params=_cparams(("arbitrary", "arbitrary", "arbitrary")),
        name="dsa_attention",
    )(slopes, qk, qk, vq, bias)


def _merge_kernel(oa_ref, wa_ref, ob_ref, wb_ref, ga_ref, gb_ref, o_ref):
    ya = jnp.dot(oa_ref[0], wa_ref[...], preferred_element_type=F32)
    yb = jnp.dot(ob_ref[0], wb_ref[...], preferred_element_type=F32)
    o_ref[0] = (ga_ref[0].astype(F32) * ya + gb_ref[0].astype(F32) * yb).astype(o_ref.dtype)


def _merge(oa, wa, ob, wb, gates):
    b, s, ka = oa.shape
    kb = ob.shape[2]
    n = wa.shape[1]
    tm, tn = _tile(s, 1024), _tile(n, 512)
    nj = n // tn
    return pl.pallas_call(
        _merge_kernel,
        grid=(b, s // tm, nj),
        in_specs=[pl.BlockSpec((1, tm, ka), lambda bi, i, j: (bi, i, 0)),
                  pl.BlockSpec((ka, tn), lambda bi, i, j: (0, j)),
                  pl.BlockSpec((1, tm, kb), lambda bi, i, j: (bi, i, 0)),
                  pl.BlockSpec((kb, tn), lambda bi, i, j: (0, j)),
                  pl.BlockSpec((1, tm, tn), lambda bi, i, j: (bi, i, j)),
                  pl.BlockSpec((1, tm, tn), lambda bi, i, j: (bi, i, nj + j))],
        out_specs=pl.BlockSpec((1, tm, tn), lambda bi, i, j: (bi, i, j)),
        out_shape=jax.ShapeDtypeStruct((b, s, n), BF16),
        compiler_params=_cparams(("arbitrary", "arbitrary", "arbitrary")),
        name="gated_merge",
    )(oa, wa, ob, wb, gates, gates)


def _resid_kernel(a_ref, w_ref, x_ref, g_ref, o_ref):
    y = jnp.dot(a_ref[0], w_ref[...], preferred_element_type=F32)
    o_ref[0] = x_ref[0] + g_ref[0] * y


def _resid_mm(a, w, x, g):
    b, s, k = a.shape
    n = w.shape[1]
    tm, tn = _tile(s, 1024), _tile(n, 512)
    return pl.pallas_call(
        _resid_kernel,
        grid=(b, s // tm, n // tn),
        in_specs=[pl.BlockSpec((1, tm, k), lambda bi, i, j: (bi, i, 0)),
                  pl.BlockSpec((k, tn), lambda bi, i, j: (0, j)),
                  pl.BlockSpec((1, tm, tn), lambda bi, i, j: (bi, i, j)),
                  pl.BlockSpec((1, 1, tn), lambda bi, i, j: (bi, 0, j))],
        out_specs=pl.BlockSpec((1, tm, tn), lambda bi, i, j: (bi, i, j)),
        out_shape=jax.ShapeDtypeStruct((b, s, n), F32),
        compiler_params=_cparams(("arbitrary", "arbitrary", "arbitrary")),
        name="out_proj_residual",
    )(a, w, x, g.reshape(b, 1, n))


def _swiglu_kernel(a_ref, w1_ref, w3_ref, o_ref):
    a = a_ref[0]
    u = jnp.dot(a, w1_ref[...], preferred_element_type=F32)
    v = jnp.dot(a, w3_ref[...], preferred_element_type=F32)
    o_ref[0] = (u * jax.nn.sigmoid(u) * v).astype(o_ref.dtype)


def _swiglu_up(a, w1, w3):
    b, s, k = a.shape
    n = w1.shape[1]
    tm, tn = _tile(s, 1024), _tile(n, 512)
    return pl.pallas_call(
        _swiglu_kernel,
        grid=(b, s // tm, n // tn),
        in_specs=[pl.BlockSpec((1, tm, k), lambda bi, i, j: (bi, i, 0)),
                  pl.BlockSpec((k, tn), lambda bi, i, j: (0, j)),
                  pl.BlockSpec((k, tn), lambda bi, i, j: (0, j))],
        out_specs=pl.BlockSpec((1, tm, tn), lambda bi, i, j: (bi, i, j)),
        out_shape=jax.ShapeDtypeStruct((b, s, n), BF16),
        compiler_params=_cparams(("arbitrary", "arbitrary", "arbitrary")),
        name="shared_swiglu_up",
    )(a, w1, w3)


def _final_kernel(a_ref, w_ref, *rest):
    yk_refs, (x_ref, g_ref, o_ref) = rest[:EXPERT_TOPK], rest[EXPERT_TOPK:]
    y = jnp.dot(a_ref[0], w_ref[...], preferred_element_type=F32)
    lo, hi = _unpack_bf16_pairs(yk_refs[0][...])
    for r in yk_refs[1:]:
        lo_r, hi_r = _unpack_bf16_pairs(r[...])
        lo, hi = lo + lo_r, hi + hi_r
    o_ref[0] = x_ref[0] + g_ref[0] * (y + jnp.concatenate([lo, hi], axis=1))


def _final(hs, w2, yk, x, g):
    b, s, k = hs.shape
    n = w2.shape[1]
    tm, tn = _tile(s, 512), YK_PACK_TILE
    assert n % tn == 0
    nti = s // tm
    t_blocks = b * nti
    yk_specs = [pl.BlockSpec((tm, tn // 2), lambda bi, i, j, kk=kk: (kk * t_blocks + bi * nti + i, j))
                for kk in range(EXPERT_TOPK)]
    return pl.pallas_call(
        _final_kernel,
        grid=(b, nti, n // tn),
        in_specs=[pl.BlockSpec((1, tm, k), lambda bi, i, j: (bi, i, 0)),
                  pl.BlockSpec((k, tn), lambda bi, i, j: (0, j))] + yk_specs + [
                  pl.BlockSpec((1, tm, tn), lambda bi, i, j: (bi, i, j)),
                  pl.BlockSpec((1, 1, tn), lambda bi, i, j: (bi, 0, j))],
        out_specs=pl.BlockSpec((1, tm, tn), lambda bi, i, j: (bi, i, j)),
        out_shape=jax.ShapeDtypeStruct((b, s, n), F32),
        compiler_params=_cparams(("arbitrary", "arbitrary", "arbitrary")),
        name="moe_combine_residual",
    )(hs, w2, *([yk] * EXPERT_TOPK), x, g.reshape(b, 1, n))


def _router_kernel(h_ref, wr_ref, b_ref, idx_ref, wgt_ref):
    tm = h_ref.shape[0]
    per = N_EXPERTS // N_GROUPS
    logits = lax.dot_general(wr_ref[...], h_ref[...], _NT, preferred_element_type=F32)
    scores = jax.nn.sigmoid(logits)
    choice = scores + b_ref[...]
    sub = lax.broadcasted_iota(I32, (per, tm), 0).astype(F32)
    gs = []
    for g in range(N_GROUPS):
        cg = choice[g * per:(g + 1) * per]
        m1 = jnp.max(cg, axis=0, keepdims=True)
        i1 = jnp.min(jnp.where(cg == m1, sub, float(per)), axis=0, keepdims=True)
        m2 = jnp.max(jnp.where(sub == i1, -jnp.inf, cg), axis=0, keepdims=True)
        gs.append(m1 + m2)
    masked = []
    for g in range(N_GROUPS):
        rank = jnp.zeros((1, tm), F32)
        for g2 in range(N_GROUPS):
            if g2 == g:
                continue
            beats = (gs[g2] > gs[g]) | ((gs[g2] == gs[g]) & (g2 < g))
            rank = rank + jnp.where(beats, 1.0, 0.0)
        masked.append(jnp.where(rank < TOPK_GROUPS, choice[g * per:(g + 1) * per], -jnp.inf))
    cm = jnp.concatenate(masked, axis=0)
    erow = lax.broadcasted_iota(I32, (N_EXPERTS, tm), 0).astype(F32)
    idxs, wts = [], []
    for _ in range(EXPERT_TOPK):
        mx = jnp.max(cm, axis=0, keepdims=True)
        ix = jnp.min(jnp.where(cm == mx, erow, float(N_EXPERTS)), axis=0, keepdims=True)
        hit = erow == ix
        idxs.append(ix)
        wts.append(jnp.sum(jnp.where(hit, scores, 0.0), axis=0, keepdims=True))
        cm = jnp.where(hit, -jnp.inf, cm)
    wt = jnp.concatenate(wts, axis=0)
    idx_ref[...] = jnp.concatenate(idxs, axis=0).astype(I32)
    wgt_ref[...] = wt / jnp.sum(wt, axis=0, keepdims=True) * ROUTED_SCALE


def _router(h2_flat, wr_t, bias_col):
    t, d = h2_flat.shape
    tm = _tile(t, 512)
    return pl.pallas_call(
        _router_kernel,
        grid=(t // tm,),
        in_specs=[pl.BlockSpec((tm, d), lambda i: (i, 0)),
                  pl.BlockSpec((N_EXPERTS, d), lambda i: (0, 0)),
                  pl.BlockSpec((N_EXPERTS, 1), lambda i: (0, 0))],
        out_specs=[pl.BlockSpec((EXPERT_TOPK, tm), lambda i: (0, i)),
                   pl.BlockSpec((EXPERT_TOPK, tm), lambda i: (0, i))],
        out_shape=[jax.ShapeDtypeStruct((EXPERT_TOPK, t), I32),
                   jax.ShapeDtypeStruct((EXPERT_TOPK, t), F32)],
        compiler_params=_cparams(("arbitrary",)),
        name="router_topk",
    )(h2_flat, wr_t, bias_col)


MOE_DMA_UNROLL = 8


def _moe_kernel(ech_ref, nact_ref, tokc_ref, tokn_ref, dst_ref, wt_ref, h_hbm, w1_ref, w3_ref, w2_ref,
                yk_hbm, xs, ys, gsem, ssem, *, dump_row0):
    del ech_ref
    c = pl.program_id(0)
    nc = pl.num_programs(0)
    nact = nact_ref[0]
    rows = MOE_CHUNK
    slot = c % 2

    def gather_copy(tok_ref, r, s):
        return pltpu.make_async_copy(h_hbm.at[pl.ds(tok_ref[0, 0, r], 1), :],
                                     xs.at[s, pl.ds(r, 1), :], gsem.at[s])

    def scatter_copy(r, dst_row):
        return pltpu.make_async_copy(ys.at[pl.ds(r, 1), :], yk_hbm.at[pl.ds(dst_row, 1), :], ssem.at[0])

    def per_row(fn):
        def body(r, carry):
            fn(r)
            return carry
        lax.fori_loop(0, rows, body, 0, unroll=MOE_DMA_UNROLL)

    def wait_scatter():
        per_row(lambda r: scatter_copy(r, 0).wait())

    @pl.when(c == 0)
    def _():
        ys[...] = jnp.zeros_like(ys)
        per_row(lambda r: scatter_copy(r, dump_row0 + r).start())
        per_row(lambda r: gather_copy(tokc_ref, r, 0).start())

    @pl.when(c < nact)
    def _():
        per_row(lambda r: gather_copy(tokn_ref, r, 1 - slot).start())
        per_row(lambda r: gather_copy(tokc_ref, r, slot).wait())

        lo, hi = _unpack_bf16_pairs(xs[slot])
        x = jnp.concatenate([lo, hi], axis=1).astype(BF16)
        u = jnp.dot(x, w1_ref[0], preferred_element_type=F32)
        v = jnp.dot(x, w3_ref[0], preferred_element_type=F32)
        hm = (u * jax.nn.sigmoid(u) * v).astype(BF16)
        wt = wt_ref[0]
        d = w2_ref.shape[2]
        half = YK_PACK_TILE // 2
        tiles = []
        for j in range(d // YK_PACK_TILE):
            yj = jnp.dot(hm, w2_ref[0, :, j * YK_PACK_TILE:(j + 1) * YK_PACK_TILE],
                         preferred_element_type=F32) * wt
            tiles.append(_pack_bf16_pairs(yj.astype(BF16)))
        wait_scatter()
        for j, tile in enumerate(tiles):
            ys[:, j * half:(j + 1) * half] = tile
        per_row(lambda r: scatter_copy(r, dst_ref[0, 0, r]).start())

    @pl.when(c == nc - 1)
    def _():
        wait_scatter()
        per_row(lambda r: gather_copy(tokc_ref, r, nact % 2).wait())


def _moe(e_chunk, n_active, tok_pad, dst_pad, w_pad, h2_packed, w1, w3, w2, n_assign):
    t, dp = h2_packed.shape
    n_chunks = tok_pad.shape[0]
    d, f = w1.shape[1], w1.shape[2]
    assert d == 2 * dp and d % YK_PACK_TILE == 0
    rows = MOE_CHUNK
    smem_blk = lambda fn: pl.BlockSpec((1, 1, rows), fn, memory_space=pltpu.SMEM)
    grid_spec = pltpu.PrefetchScalarGridSpec(
        num_scalar_prefetch=2,
        grid=(n_chunks,),
        in_specs=[smem_blk(lambda c, ech, na: (c, 0, 0)),
                  smem_blk(lambda c, ech, na: (jnp.minimum(c + 1, n_chunks - 1), 0, 0)),
                  smem_blk(lambda c, ech, na: (c, 0, 0)),
                  pl.BlockSpec((1, rows, 1), lambda c, ech, na: (c, 0, 0)),
                  pl.BlockSpec(memory_space=pl.ANY),
                  pl.BlockSpec((1, d, f), lambda c, ech, na: (ech[c], 0, 0)),
                  pl.BlockSpec((1, d, f), lambda c, ech, na: (ech[c], 0, 0)),
                  pl.BlockSpec((1, f, d), lambda c, ech, na: (ech[c], 0, 0))],
        out_specs=pl.BlockSpec(memory_space=pl.ANY),
        scratch_shapes=[pltpu.VMEM((2, rows, dp), U32),
                        pltpu.VMEM((rows, dp), U32),
                        pltpu.SemaphoreType.DMA((2,)),
                        pltpu.SemaphoreType.DMA((1,))],
    )
    return pl.pallas_call(
        functools.partial(_moe_kernel, dump_row0=n_assign),
        grid_spec=grid_spec,
        out_shape=jax.ShapeDtypeStruct((n_assign + rows, dp), U32),
        compiler_params=_cparams(("arbitrary",)),
        name="routed_experts",
    )(e_chunk, n_active, tok_pad, tok_pad, dst_pad, w_pad, h2_packed, w1, w3, w2)


def _dispatch(idx_t, wgt_t):
    k, t = idx_t.shape
    e, c = N_EXPERTS, MOE_CHUNK
    m = k * t
    p = m + e * c
    n_chunks = p // c
    e_flat = idx_t.reshape(m)
    w_flat = wgt_t.reshape(m)
    order = jnp.argsort(e_flat).astype(I32)
    counts = jnp.sum((e_flat[None, :] == jnp.arange(e, dtype=I32)[:, None]).astype(I32), axis=1)
    starts = jnp.cumsum(counts) - counts
    padded = (counts + c - 1) // c * c
    pends = jnp.cumsum(padded)
    pstarts = pends - padded
    pos = jnp.arange(p, dtype=I32)
    e_of_p = jnp.minimum(jnp.sum((pos[:, None] >= pends[None, :]).astype(I32), axis=1), e - 1)
    off = pos - pstarts[e_of_p]
    valid = (off < counts[e_of_p]) & (pos < pends[-1])
    src = jnp.where(valid, starts[e_of_p] + off, 0)
    a_p = order[src]
    tok_pad = jnp.where(valid, a_p % t, 0).astype(I32)
    w_pad = jnp.where(valid, w_flat[a_p], 0.0).astype(F32)
    dst_pad = jnp.where(valid, a_p, m + pos % c).astype(I32)
    e_chunk = e_of_p[::c].astype(I32)
    n_active = (pends[-1] // c).astype(I32).reshape(1)
    return (e_chunk, n_active, tok_pad.reshape(n_chunks, 1, c), dst_pad.reshape(n_chunks, 1, c),
            w_pad.reshape(n_chunks, c, 1), m)


def _alibi_slopes(n_heads):
    return 2.0 ** (-8.0 * jnp.arange(1, n_heads + 1, dtype=F32) / n_heads)


def _bf16_parts(v):
    p1 = v.astype(BF16).astype(F32)
    p2 = (v - p1).astype(BF16).astype(F32)
    p3 = (v - p1 - p2).astype(BF16).astype(F32)
    return jnp.stack([p1, p2, p3], axis=1)


def _layer(x, c_pad, ln1_gain, ln2_gain, w_ada, b_ada, w_in, moba_q_gain, moba_k_gain, dsa_q_gain,
           dsa_k_gain, idx_k_gain, w_proj_moba, w_proj_dsa, w_out, w_router, router_bias,
           w1_experts, w3_experts, w2_experts, w1_shared, w3_shared, w2_shared):
    b, s, d = x.shape
    t = b * s
    mw = MOBA_HEADS * HEAD_DIM
    qw = DSA_HEADS * HEAD_DIM
    kvw = DSA_KV_HEADS * HEAD_DIM
    iw = IDX_HEADS * IDX_DIM

    mod = _ada(c_pad, w_ada, b_ada.reshape(1, -1))[:b]
    sh1, sc1, g1, sh2, sc2, g2 = jnp.split(mod, 6, axis=-1)

    (h,) = _normmod(x, ln1_gain, sc1, sh1, with_packed=False)

    o_qa, o_ka, o_va = 0, mw, 2 * mw
    o_qb = 3 * mw
    o_kb = o_qb + qw
    o_vb = o_kb + kvw
    o_qi = o_vb + kvw
    o_ki = o_qi + iw
    o_wi = o_ki + IDX_DIM
    o_ga = o_wi + IDX_HEADS
    o_gb = o_ga + d
    cols = lambda a, n: w_in[:, a:a + n]

    scale = HEAD_DIM ** -0.5
    w_qk = jnp.concatenate([cols(o_qa, mw), cols(o_ka, mw), cols(o_qb, qw), cols(o_kb, kvw)], axis=1).astype(BF16)
    g_qk = jnp.concatenate([jnp.tile(moba_q_gain * scale, MOBA_HEADS), jnp.tile(moba_k_gain, MOBA_HEADS),
                            jnp.tile(dsa_q_gain * scale, DSA_HEADS), jnp.tile(dsa_k_gain, DSA_KV_HEADS)])
    qk = _proj(h, w_qk, g_qk.reshape(1, -1), "headnorm", BF16)
    w_vq = jnp.concatenate([cols(o_va, mw), cols(o_vb, kvw), cols(o_qi, iw)], axis=1).astype(BF16)
    vq = _proj(h, w_vq, jnp.zeros((1, w_vq.shape[1]), F32), "plain", BF16)
    w_gate = cols(o_ga, 2 * d).astype(BF16)
    gates = _proj(h, w_gate, jnp.zeros((1, 2 * d), F32), "sigmoid", BF16)
    w_ix = jnp.pad(cols(o_ki, IDX_DIM + IDX_HEADS), ((0, 0), (0, LANES - IDX_DIM - IDX_HEADS))).astype(BF16)
    aux_ix = jnp.concatenate([idx_k_gain, jnp.full((IDX_HEADS,), IDX_HEADS ** -0.5 * IDX_DIM ** -0.5, F32),
                              jnp.zeros((LANES - IDX_DIM - IDX_HEADS,), F32)])
    kiw = _proj(h, w_ix, aux_ix.reshape(1, -1), "indexer", F32, tn_pref=LANES)

    hd = HEAD_DIM
    o_a = _moba(_bf16_parts(_alibi_slopes(MOBA_HEADS)), qk, vq, q_col0=0, k_col0=mw // hd, v_col0=0)

    qi_t = vq[:, :, mw + kvw:].reshape(b, s, IDX_HEADS, IDX_DIM).transpose(0, 2, 1, 3)
    ki = kiw[:, :, :IDX_DIM].astype(BF16)
    wi = kiw[:, :, IDX_DIM:IDX_DIM + IDX_HEADS]
    bias = _dsa_select(qi_t, ki, wi)
    o_b = _dsa_attn(_alibi_slopes(DSA_HEADS), qk, vq, bias, q_col0=2 * mw // hd,
                    k_col0=(2 * mw + qw) // hd, v_col0=mw // hd)

    merged = _merge(o_a, w_proj_moba.astype(BF16), o_b, w_proj_dsa.astype(BF16), gates)
    x1 = _resid_mm(merged, w_out.astype(BF16), x, g1)

    h2, h2_packed = _normmod(x1, ln2_gain, sc2, sh2, with_packed=True)
    wr_t = w_router.T.astype(BF16)
    idx_t, wgt_t = _router(h2.reshape(t, d), wr_t, router_bias.reshape(-1, 1))
    e_chunk, n_active, tok_pad, dst_pad, w_pad, n_assign = _dispatch(idx_t, wgt_t)
    yk = _moe(e_chunk, n_active, tok_pad, dst_pad, w_pad, h2_packed.reshape(t, d // 2),
              w1_experts.astype(BF16), w3_experts.astype(BF16), w2_experts.astype(BF16), n_assign)
    hs = _swiglu_up(h2, w1_shared.astype(BF16), w3_shared.astype(BF16))
    return _final(hs, w2_shared.astype(BF16), yk, x1, g2)


def kernel(x, c, ln1_gain, ln2_gain, w_ada, b_ada, w_in, moba_q_gain, moba_k_gain, dsa_q_gain, dsa_k_gain, idx_k_gain, w_proj_moba, w_proj_dsa, w_out, w_router, router_bias, w1_experts, w3_experts, w2_experts, w1_shared, w3_shared, w2_shared):
    depth = w_ada.shape[0]
    b = x.shape[0]
    c_pad = jnp.pad(c, ((0, (-b) % 8), (0, 0)))
    params = (ln1_gain, ln2_gain, w_ada, b_ada, w_in, moba_q_gain, moba_k_gain, dsa_q_gain, dsa_k_gain,
              idx_k_gain, w_proj_moba, w_proj_dsa, w_out, w_router, router_bias, w1_experts, w3_experts,
              w2_experts, w1_shared, w3_shared, w2_shared)
    for l in range(depth):
        x = _layer(x, c_pad, *[p[l] for p in params])
    return x
```

```python
import functools

import jax
import jax.numpy as jnp
from jax import lax
from jax.experimental import pallas as pl
from jax.experimental.pallas import tpu as pltpu

F32 = jnp.float32
BF16 = jnp.bfloat16
I32 = jnp.int32
U32 = jnp.uint32

HEAD_DIM = 128
MOBA_HEADS = 16
MOBA_BLOCK = 256
MOBA_TOPK = 3
DSA_HEADS = 16
DSA_KV_HEADS = 4
DSA_TOPK_MAX = 256
IDX_HEADS = 32
IDX_DIM = 64
N_EXPERTS = 64
EXPERT_TOPK = 8
N_GROUPS = 8
TOPK_GROUPS = 4
ROUTED_SCALE = 2.5
NORM_EPS = 1e-6

LANES = 128
V7X_VMEM_LIMIT_BYTES = 56 * 1024 * 1024

MOE_CHUNK = 256
YK_PACK_TILE = 512
NEG = -1e30
INT_MIN = -2 ** 31
LOG2E = 1.4426950408889634

_NT = (((1,), (1,)), ((), ()))


def _cparams(sem):
    return pltpu.CompilerParams(dimension_semantics=sem, vmem_limit_bytes=V7X_VMEM_LIMIT_BYTES)


def _tile(n, pref):
    if n <= pref:
        return n
    t = pref - pref % LANES
    while t > LANES and n % t:
        t -= LANES
    assert n % t == 0, (n, pref)
    return t


def _ada_kernel(c_ref, w_ref, b_ref, o_ref):
    c = c_ref[...]
    s = c * jax.nn.sigmoid(c)
    acc = jnp.dot(s.astype(BF16), w_ref[...].astype(BF16), preferred_element_type=F32)
    o_ref[...] = acc + b_ref[...]


def _ada(c_pad, w_ada, b_ada):
    rows, d = c_pad.shape
    n = w_ada.shape[1]
    tn = _tile(n, 512)
    return pl.pallas_call(
        _ada_kernel,
        grid=(n // tn,),
        in_specs=[pl.BlockSpec((rows, d), lambda j: (0, 0)),
                  pl.BlockSpec((d, tn), lambda j: (0, j)),
                  pl.BlockSpec((1, tn), lambda j: (0, j))],
        out_specs=pl.BlockSpec((rows, tn), lambda j: (0, j)),
        out_shape=jax.ShapeDtypeStruct((rows, n), F32),
        compiler_params=_cparams(("arbitrary",)),
        name="ada_mod",
    )(c_pad, w_ada, b_ada)


def _pack_bf16_pairs(v):
    n = v.shape[1] // 2
    bits = lax.bitcast_convert_type(v.astype(F32), U32)
    return (bits[:, :n] >> 16) | bits[:, n:]


def _unpack_bf16_pairs(w):
    lo = lax.bitcast_convert_type(w << 16, F32)
    hi = lax.bitcast_convert_type(w & jnp.uint32(0xFFFF0000), F32)
    return lo, hi


def _normmod_kernel(x_ref, gain_ref, sc_ref, sh_ref, *o_refs):
    x = x_ref[0]
    y = x * lax.rsqrt(jnp.mean(x * x, axis=-1, keepdims=True) + NORM_EPS) * gain_ref[...]
    h = (y * (1.0 + sc_ref[0]) + sh_ref[0]).astype(BF16)
    o_refs[0][0] = h
    if len(o_refs) > 1:
        o_refs[1][0] = _pack_bf16_pairs(h)


def _normmod(x, gain, sc, sh, with_packed):
    b, s, d = x.shape
    ts = _tile(s, 256)
    blk = pl.BlockSpec((1, ts, d), lambda bi, i: (bi, i, 0))
    vec = pl.BlockSpec((1, 1, d), lambda bi, i: (bi, 0, 0))
    out_shape = [jax.ShapeDtypeStruct((b, s, d), BF16)]
    out_specs = [blk]
    if with_packed:
        out_shape.append(jax.ShapeDtypeStruct((b, s, d // 2), U32))
        out_specs.append(pl.BlockSpec((1, ts, d // 2), lambda bi, i: (bi, i, 0)))
    return pl.pallas_call(
        _normmod_kernel,
        grid=(b, s // ts),
        in_specs=[blk, pl.BlockSpec((1, d), lambda bi, i: (0, 0)), vec, vec],
        out_specs=out_specs,
        out_shape=out_shape,
        compiler_params=_cparams(("arbitrary", "arbitrary")),
        name="norm_modulate",
    )(x, gain.reshape(1, d), sc.reshape(b, 1, d), sh.reshape(b, 1, d))


def _proj_kernel(a_ref, w_ref, aux_ref, o_ref, *, mode):
    acc = jnp.dot(a_ref[0], w_ref[...], preferred_element_type=F32)
    tn = acc.shape[1]
    if mode == "plain":
        o_ref[0] = acc.astype(o_ref.dtype)
    elif mode == "sigmoid":
        o_ref[0] = jax.nn.sigmoid(acc).astype(o_ref.dtype)
    elif mode == "headnorm":
        for g in range(tn // HEAD_DIM):
            sl = slice(g * HEAD_DIM, (g + 1) * HEAD_DIM)
            blk = acc[:, sl]
            ms = jnp.mean(blk * blk, axis=-1, keepdims=True)
            o_ref[0, :, sl] = (blk * lax.rsqrt(ms + NORM_EPS) * aux_ref[:, sl]).astype(o_ref.dtype)
    elif mode == "indexer":
        lane = lax.broadcasted_iota(I32, acc.shape, 1)
        is_key = lane < IDX_DIM
        ms = jnp.sum(jnp.where(is_key, acc * acc, 0.0), axis=-1, keepdims=True) / IDX_DIM
        o_ref[0] = acc * jnp.where(is_key, lax.rsqrt(ms + NORM_EPS), 1.0) * aux_ref[...]
    else:
        raise ValueError(mode)


def _proj(a, w, aux, mode, out_dtype, tm_pref=1024, tn_pref=512):
    b, s, k = a.shape
    n = w.shape[1]
    tm, tn = _tile(s, tm_pref), _tile(n, tn_pref)
    return pl.pallas_call(
        functools.partial(_proj_kernel, mode=mode),
        grid=(b, s // tm, n // tn),
        in_specs=[pl.BlockSpec((1, tm, k), lambda bi, i, j: (bi, i, 0)),
                  pl.BlockSpec((k, tn), lambda bi, i, j: (0, j)),
                  pl.BlockSpec((1, tn), lambda bi, i, j: (0, j))],
        out_specs=pl.BlockSpec((1, tm, tn), lambda bi, i, j: (bi, i, j)),
        out_shape=jax.ShapeDtypeStruct((b, s, n), out_dtype),
        compiler_params=_cparams(("arbitrary", "arbitrary", "arbitrary")),
        name="proj_" + mode,
    )(a, w, aux)


AUG_ALIBI = 0
AUG_SEL = 8
MOBA_QBLOCKS = 2


def _moba_kernel(sp_ref, q_ref, k_ref, v_ref, o_ref, kaug_sc, vt_sc, kmean_sc, *, nb, gt, qb):
    h = pl.program_id(1)
    ip = pl.program_id(2)
    blk = MOBA_BLOCK
    tq = qb * blk
    nbp = kmean_sc.shape[0]
    gblocks = gt // blk

    @pl.when(ip == 0)
    def _():
        kmean_sc[...] = jnp.zeros_like(kmean_sc)
        lane = lax.broadcasted_iota(I32, (blk, LANES), 1)
        row = lax.broadcasted_iota(I32, (blk, LANES), 0)
        for n in range(nb):
            rows = slice(n * blk, (n + 1) * blk)
            kb = k_ref[0, rows, :]
            kmean_sc[n:n + 1, :] = jnp.mean(kb.astype(F32), axis=0, keepdims=True)
            aug = jnp.where(lane == AUG_SEL + n, 1.0, _alibi_key_aug(n * blk + row, lane))
            kaug_sc[rows, :HEAD_DIM] = kb
            kaug_sc[rows, HEAD_DIM:] = aug.astype(BF16)
            vt_sc[:, rows] = v_ref[0, rows, :].astype(F32).T.astype(BF16)

    q = q_ref[0]
    gate_t = lax.dot_general(kmean_sc[...], q.astype(F32), _NT, preferred_element_type=F32,
                             precision=lax.Precision.HIGHEST)
    r_iota = lax.broadcasted_iota(I32, gate_t.shape, 0)
    assert blk & (blk - 1) == 0
    own = ip * qb + (lax.broadcasted_iota(I32, gate_t.shape, 1) >> (blk.bit_length() - 1))
    rank = jnp.zeros(gate_t.shape, I32)
    for m in range(nb):
        gm = gate_t[m:m + 1, :]
        beats = ((gm > gate_t) | ((gm == gate_t) & (m < r_iota))) & (m < own)
        rank = rank + jnp.where(beats, 1, 0)
    keep_t = ((r_iota < own) & (rank < MOBA_TOPK)) | (r_iota == own)
    selb_t = jnp.where(keep_t, 0.0, NEG)

    augq_t = jnp.concatenate([_alibi_query_aug(sp_ref, h, (8, tq), 0), selb_t,
                              jnp.zeros((LANES - 8 - nbp, tq), F32)], axis=0)
    qa = jnp.concatenate([q, augq_t.T.astype(BF16)], axis=1)

    t_pos = ip * tq + lax.broadcasted_iota(I32, (gt, tq), 1)
    s_off = lax.broadcasted_iota(I32, (gt, tq), 0)

    def group(gi, carry, causal):
        m_prev, l_prev, acc_prev = carry
        g0 = pl.multiple_of(gi * gt, gt)
        st = lax.dot_general(kaug_sc[pl.ds(g0, gt), :], qa, _NT, preferred_element_type=F32)
        if causal:
            st = jnp.where(g0 + s_off <= t_pos, st, NEG)
        m_new = jnp.maximum(m_prev, jnp.max(st, axis=0, keepdims=True))
        alpha = jnp.exp2(m_prev - m_new)
        p = jnp.exp2(st - m_new)
        l_new = alpha * l_prev + jnp.sum(p, axis=0, keepdims=True)
        pv = jnp.dot(vt_sc[:, pl.ds(g0, gt)], p.astype(BF16), preferred_element_type=F32)
        return m_new, l_new, alpha * acc_prev + pv

    init = (jnp.full((1, tq), NEG, F32), jnp.zeros((1, tq), F32), jnp.zeros((HEAD_DIM, tq), F32))
    n_full = (ip * qb) // gblocks
    carry = lax.fori_loop(0, n_full, lambda gi, c: group(gi, c, False), init)
    _, l_fin, acc_fin = group(n_full, carry, True)
    o_ref[0] = (acc_fin / l_fin).T.astype(o_ref.dtype)


def _moba(slope_parts, qk, vq, q_col0, k_col0, v_col0):
    b, s, _ = qk.shape
    assert s % MOBA_BLOCK == 0 and s <= 4096
    nb = s // MOBA_BLOCK
    nbp = -(-nb // 8) * 8
    assert AUG_SEL + nbp <= LANES
    gt = min(s, 4 * MOBA_BLOCK)
    qb = MOBA_QBLOCKS
    assert s % gt == 0 and nb % qb == 0 and (gt // MOBA_BLOCK) % qb == 0
    tq = qb * MOBA_BLOCK
    return pl.pallas_call(
        functools.partial(_moba_kernel, nb=nb, gt=gt, qb=qb),
        grid=(b, MOBA_HEADS, nb // qb),
        in_specs=[pl.BlockSpec(memory_space=pltpu.SMEM),
                  pl.BlockSpec((1, tq, HEAD_DIM), lambda bi, h, i: (bi, i, q_col0 + h)),
                  pl.BlockSpec((1, s, HEAD_DIM), lambda bi, h, i: (bi, 0, k_col0 + h)),
                  pl.BlockSpec((1, s, HEAD_DIM), lambda bi, h, i: (bi, 0, v_col0 + h))],
        out_specs=pl.BlockSpec((1, tq, HEAD_DIM), lambda bi, h, i: (bi, i, h)),
        out_shape=jax.ShapeDtypeStruct((b, s, MOBA_HEADS * HEAD_DIM), BF16),
        scratch_shapes=[pltpu.VMEM((s, 2 * HEAD_DIM), BF16),
                        pltpu.VMEM((HEAD_DIM, s), BF16),
                        pltpu.VMEM((nbp, HEAD_DIM), F32)],
        compiler_params=_cparams(("arbitrary", "arbitrary", "arbitrary")),
        name="moba_attention",
    )(slope_parts, qk, qk, vq)


def _dsa_select_kernel(qi_ref, ki_ref, wi_ref, bias_ref, key_sc, *, tq, tk, nsel, hgroup):
    j = pl.program_id(1)
    t0 = j * tq
    nkc = (t0 + tq + tk - 1) // tk
    n_hg = IDX_HEADS // hgroup

    bias_ref[...] = jnp.full(bias_ref.shape, NEG, bias_ref.dtype)
    w = wi_ref[0]
    row_t = t0 + lax.broadcasted_iota(I32, (tq, tk), 0)
    col_i = lax.broadcasted_iota(I32, (tq, tk), 1)

    def score_chunk(c, _):
        c0 = pl.multiple_of(c * tk, tk)
        kc = ki_ref[0, pl.ds(c0, tk), :]
        acc = jnp.zeros((tq, tk), F32)
        for g in range(n_hg):
            qg = qi_ref[0, g * hgroup:(g + 1) * hgroup].reshape(hgroup * tq, IDX_DIM)
            d = lax.dot_general(qg, kc, _NT, preferred_element_type=F32)
            for hh in range(hgroup):
                hd = g * hgroup + hh
                acc = acc + w[:, hd:hd + 1] * jnp.maximum(d[hh * tq:(hh + 1) * tq], 0.0)
        bits = lax.bitcast_convert_type(acc, I32)
        key = jnp.where(bits >= 0, bits, bits ^ 0x7FFFFFFF)
        key = jnp.where(c0 + col_i <= row_t, key, INT_MIN)
        key_sc[:, pl.ds(c0, tk)] = key
        return 0

    lax.fori_loop(0, nkc, score_chunk, 0)

    def bit_body(bi, prefix):
        cand = prefix + jnp.left_shift(jnp.int32(1), 31 - bi)

        def cnt_chunk(c, cnt):
            blk = key_sc[:, pl.ds(pl.multiple_of(c * tk, tk), tk)]
            ge = jnp.where(blk >= cand, 1.0, 0.0)
            for g in range(tk // LANES):
                cnt = cnt + ge[:, g * LANES:(g + 1) * LANES]
            return cnt

        cnt = lax.fori_loop(0, nkc, cnt_chunk, jnp.zeros((tq, LANES), F32))
        total = jnp.sum(cnt, axis=-1, keepdims=True)
        return jnp.where(total >= nsel, cand, prefix)

    thr = lax.fori_loop(0, 32, bit_body, jnp.full((tq, 1), INT_MIN, I32))

    def out_chunk(c, _):
        c0 = pl.multiple_of(c * tk, tk)
        blk = key_sc[:, pl.ds(c0, tk)]
        keep = (blk >= thr) & (c0 + col_i <= row_t)
        bias_ref[0, :, pl.ds(c0, tk)] = jnp.where(keep, 0.0, NEG).astype(bias_ref.dtype)
        return 0

    lax.fori_loop(0, nkc, out_chunk, 0)


def _dsa_select(qi_t, ki, wi):
    b, _, s, _ = qi_t.shape
    tq = _tile(s, 128)
    tk = _tile(s, 512)
    nsel = min(DSA_TOPK_MAX, s // 4)
    return pl.pallas_call(
        functools.partial(_dsa_select_kernel, tq=tq, tk=tk, nsel=nsel, hgroup=8),
        grid=(b, s // tq),
        in_specs=[pl.BlockSpec((1, IDX_HEADS, tq, IDX_DIM), lambda bi, j: (bi, 0, j, 0)),
                  pl.BlockSpec((1, s, IDX_DIM), lambda bi, j: (bi, 0, 0)),
                  pl.BlockSpec((1, tq, IDX_HEADS), lambda bi, j: (bi, j, 0))],
        out_specs=pl.BlockSpec((1, tq, s), lambda bi, j: (bi, j, 0)),
        out_shape=jax.ShapeDtypeStruct((b, s, s), BF16),
        scratch_shapes=[pltpu.VMEM((tq, s), I32)],
        compiler_params=_cparams(("arbitrary", "arbitrary")),
        name="dsa_select",
    )(qi_t, ki, wi)


def _alibi_key_aug(pos, lane):
    alibi = jnp.where((lane & 1) == 0, ((pos >> 4) << 4).astype(F32), (pos & 15).astype(F32))
    return jnp.where(lane < AUG_ALIBI + 6, alibi, 0.0)


def _alibi_query_aug(sp_ref, h, shape, axis):
    ix = lax.broadcasted_iota(I32, shape, axis)
    return jnp.where(ix < 2, sp_ref[h, 0], jnp.where(ix < 4, sp_ref[h, 1], jnp.where(ix < 6, sp_ref[h, 2], 0.0)))


def _dsa_attn_kernel(sp_ref, q_ref, k_ref, v_ref, bias_ref, o_ref, kaug_sc, *, tq, tk, grp):
    g = pl.program_id(1)
    j = pl.program_id(2)
    t0 = j * tq
    nk = (t0 + tq + tk - 1) // tk
    rows = grp * tq
    dh = HEAD_DIM
    s_len = kaug_sc.shape[0]

    @pl.when(j == 0)
    def _():
        lane = lax.broadcasted_iota(I32, (tk, LANES), 1)
        row = lax.broadcasted_iota(I32, (tk, LANES), 0)
        for n in range(s_len // tk):
            rs = slice(n * tk, (n + 1) * tk)
            kaug_sc[rs, :dh] = k_ref[0, rs, :]
            kaug_sc[rs, dh:] = _alibi_key_aug(n * tk + row, lane).astype(BF16)

    q = jnp.concatenate(
        [jnp.concatenate([q_ref[0, :, r * dh:(r + 1) * dh],
                          _alibi_query_aug(sp_ref, g * grp + r, (tq, LANES), 1).astype(BF16)], axis=1)
         for r in range(grp)], axis=0)

    def body(c, carry):
        m_prev, l_prev, acc_prev = carry
        c0 = pl.multiple_of(c * tk, tk)
        s = lax.dot_general(q, kaug_sc[pl.ds(c0, tk), :], _NT, preferred_element_type=F32)
        mb = bias_ref[0, :, pl.ds(c0, tk)].astype(F32)
        lg = jnp.concatenate([s[r * tq:(r + 1) * tq] + mb for r in range(grp)], axis=0)
        m_new = jnp.maximum(m_prev, jnp.max(lg, axis=-1, keepdims=True))
        alpha = jnp.exp2(m_prev - m_new)
        p = jnp.exp2(lg - m_new)
        l_new = alpha * l_prev + jnp.sum(p, axis=-1, keepdims=True)
        pv = jnp.dot(p.astype(BF16), v_ref[0, pl.ds(c0, tk), :], preferred_element_type=F32)
        return m_new, l_new, alpha * acc_prev + pv

    init = (jnp.full((rows, 1), NEG, F32), jnp.zeros((rows, 1), F32), jnp.zeros((rows, dh), F32))
    _, l_fin, acc_fin = lax.fori_loop(0, nk, body, init)
    out = acc_fin / l_fin
    for r in range(grp):
        o_ref[0, :, r * dh:(r + 1) * dh] = out[r * tq:(r + 1) * tq].astype(o_ref.dtype)


def _dsa_attn(slope_parts, qk, vq, bias, q_col0, k_col0, v_col0):
    b, s, _ = qk.shape
    assert s <= 4096
    grp = DSA_HEADS // DSA_KV_HEADS
    gw = grp * HEAD_DIM
    assert (q_col0 * HEAD_DIM) % gw == 0
    q_blk0 = q_col0 * HEAD_DIM // gw
    tq = _tile(s, 128)
    tk = _tile(s, 512)
    return pl.pallas_call(
        functools.partial(_dsa_attn_kernel, tq=tq, tk=tk, grp=grp),
        grid=(b, DSA_KV_HEADS, s // tq),
        in_specs=[pl.BlockSpec(memory_space=pltpu.SMEM),
                  pl.BlockSpec((1, tq, gw), lambda bi, g, j: (bi, j, q_blk0 + g)),
                  pl.BlockSpec((1, s, HEAD_DIM), lambda bi, g, j: (bi, 0, k_col0 + g)),
                  pl.BlockSpec((1, s, HEAD_DIM), lambda bi, g, j: (bi, 0, v_col0 + g)),
                  pl.BlockSpec((1, tq, s), lambda bi, g, j: (bi, j, 0))],
        out_specs=pl.BlockSpec((1, tq, gw), lambda bi, g, j: (bi, j, g)),
        out_shape=jax.ShapeDtypeStruct((b, s, DSA_HEADS * HEAD_DIM), BF16),
        scratch_shapes=[pltpu.VMEM((s, 2 * HEAD_DIM), BF16)],
        compiler_params=_cparams(("arbitrary", "arbitrary", "arbitrary")),
        name="dsa_attention",
    )(slope_parts, qk, qk, vq, bias)


def _merge_kernel(oa_ref, wa_ref, ob_ref, wb_ref, ga_ref, gb_ref, o_ref):
    ya = jnp.dot(oa_ref[0], wa_ref[...], preferred_element_type=F32)
    yb = jnp.dot(ob_ref[0], wb_ref[...], preferred_element_type=F32)
    o_ref[0] = (ga_ref[0].astype(F32) * ya + gb_ref[0].astype(F32) * yb).astype(o_ref.dtype)


def _merge(oa, wa, ob, wb, gates):
    b, s, ka = oa.shape
    kb = ob.shape[2]
    n = wa.shape[1]
    tm, tn = _tile(s, 1024), _tile(n, 512)
    nj = n // tn
    return pl.pallas_call(
        _merge_kernel,
        grid=(b, s // tm, nj),
        in_specs=[pl.BlockSpec((1, tm, ka), lambda bi, i, j: (bi, i, 0)),
                  pl.BlockSpec((ka, tn), lambda bi, i, j: (0, j)),
                  pl.BlockSpec((1, tm, kb), lambda bi, i, j: (bi, i, 0)),
                  pl.BlockSpec((kb, tn), lambda bi, i, j: (0, j)),
                  pl.BlockSpec((1, tm, tn), lambda bi, i, j: (bi, i, j)),
                  pl.BlockSpec((1, tm, tn), lambda bi, i, j: (bi, i, nj + j))],
        out_specs=pl.BlockSpec((1, tm, tn), lambda bi, i, j: (bi, i, j)),
        out_shape=jax.ShapeDtypeStruct((b, s, n), BF16),
        compiler_params=_cparams(("arbitrary", "arbitrary", "arbitrary")),
        name="gated_merge",
    )(oa, wa, ob, wb, gates, gates)


def _resid_kernel(a_ref, w_ref, x_ref, g_ref, o_ref):
    y = jnp.dot(a_ref[0], w_ref[...], preferred_element_type=F32)
    o_ref[0] = x_ref[0] + g_ref[0] * y


def _resid_mm(a, w, x, g):
    b, s, k = a.shape
    n = w.shape[1]
    tm, tn = _tile(s, 1024), _tile(n, 512)
    return pl.pallas_call(
        _resid_kernel,
        grid=(b, s // tm, n // tn),
        in_specs=[pl.BlockSpec((1, tm, k), lambda bi, i, j: (bi, i, 0)),
                  pl.BlockSpec((k, tn), lambda bi, i, j: (0, j)),
                  pl.BlockSpec((1, tm, tn), lambda bi, i, j: (bi, i, j)),
                  pl.BlockSpec((1, 1, tn), lambda bi, i, j: (bi, 0, j))],
        out_specs=pl.BlockSpec((1, tm, tn), lambda bi, i, j: (bi, i, j)),
        out_shape=jax.ShapeDtypeStruct((b, s, n), F32),
        compiler_params=_cparams(("arbitrary", "arbitrary", "arbitrary")),
        name="out_proj_residual",
    )(a, w, x, g.reshape(b, 1, n))


def _swiglu_kernel(a_ref, w1_ref, w3_ref, o_ref):
    a = a_ref[0]
    u = jnp.dot(a, w1_ref[...], preferred_element_type=F32)
    v = jnp.dot(a, w3_ref[...], preferred_element_type=F32)
    o_ref[0] = (u * jax.nn.sigmoid(u) * v).astype(o_ref.dtype)


def _swiglu_up(a, w1, w3):
    b, s, k = a.shape
    n = w1.shape[1]
    tm, tn = _tile(s, 1024), _tile(n, 512)
    return pl.pallas_call(
        _swiglu_kernel,
        grid=(b, s // tm, n // tn),
        in_specs=[pl.BlockSpec((1, tm, k), lambda bi, i, j: (bi, i, 0)),
                  pl.BlockSpec((k, tn), lambda bi, i, j: (0, j)),
                  pl.BlockSpec((k, tn), lambda bi, i, j: (0, j))],
        out_specs=pl.BlockSpec((1, tm, tn), lambda bi, i, j: (bi, i, j)),
        out_shape=jax.ShapeDtypeStruct((b, s, n), BF16),
        compiler_params=_cparams(("arbitrary", "arbitrary", "arbitrary")),
        name="shared_swiglu_up",
    )(a, w1, w3)


def _final_kernel(a_ref, w_ref, *rest):
    yk_refs, (rw_ref, x_ref, g_ref, o_ref) = rest[:EXPERT_TOPK], rest[EXPERT_TOPK:]
    y = jnp.dot(a_ref[0], w_ref[...], preferred_element_type=F32)
    lo = hi = None
    for kk, r in enumerate(yk_refs):
        lo_r, hi_r = _unpack_bf16_pairs(r[...])
        wk = rw_ref[:, kk:kk + 1]
        lo = lo_r * wk if lo is None else lo + lo_r * wk
        hi = hi_r * wk if hi is None else hi + hi_r * wk
    o_ref[0] = x_ref[0] + g_ref[0] * (y + jnp.concatenate([lo, hi], axis=1))


def _final(hs, w2, yk, route_w, x, g):
    b, s, k = hs.shape
    n = w2.shape[1]
    tm, tn = _tile(s, 512), YK_PACK_TILE
    assert n % tn == 0
    nti = s // tm
    t_blocks = b * nti
    yk_specs = [pl.BlockSpec((tm, tn // 2), lambda bi, i, j, kk=kk: (kk * t_blocks + bi * nti + i, j))
                for kk in range(EXPERT_TOPK)]
    return pl.pallas_call(
        _final_kernel,
        grid=(b, nti, n // tn),
        in_specs=[pl.BlockSpec((1, tm, k), lambda bi, i, j: (bi, i, 0)),
                  pl.BlockSpec((k, tn), lambda bi, i, j: (0, j))] + yk_specs + [
                  pl.BlockSpec((tm, EXPERT_TOPK), lambda bi, i, j: (bi * nti + i, 0)),
                  pl.BlockSpec((1, tm, tn), lambda bi, i, j: (bi, i, j)),
                  pl.BlockSpec((1, 1, tn), lambda bi, i, j: (bi, 0, j))],
        out_specs=pl.BlockSpec((1, tm, tn), lambda bi, i, j: (bi, i, j)),
        out_shape=jax.ShapeDtypeStruct((b, s, n), F32),
        compiler_params=_cparams(("arbitrary", "arbitrary", "arbitrary")),
        name="moe_combine_residual",
    )(hs, w2, *([yk] * EXPERT_TOPK), route_w, x, g.reshape(b, 1, n))


def _router_kernel(h_ref, wr_ref, b_ref, idx_ref, wgt_ref):
    tm = h_ref.shape[0]
    per = N_EXPERTS // N_GROUPS
    logits = lax.dot_general(wr_ref[...], h_ref[...], _NT, preferred_element_type=F32)
    scores = jax.nn.sigmoid(logits)
    choice = scores + b_ref[...]
    sub = lax.broadcasted_iota(I32, (per, tm), 0).astype(F32)
    gs = []
    for g in range(N_GROUPS):
        cg = choice[g * per:(g + 1) * per]
        m1 = jnp.max(cg, axis=0, keepdims=True)
        i1 = jnp.min(jnp.where(cg == m1, sub, float(per)), axis=0, keepdims=True)
        m2 = jnp.max(jnp.where(sub == i1, -jnp.inf, cg), axis=0, keepdims=True)
        gs.append(m1 + m2)
    masked = []
    for g in range(N_GROUPS):
        rank = jnp.zeros((1, tm), F32)
        for g2 in range(N_GROUPS):
            if g2 == g:
                continue
            beats = (gs[g2] > gs[g]) | ((gs[g2] == gs[g]) & (g2 < g))
            rank = rank + jnp.where(beats, 1.0, 0.0)
        masked.append(jnp.where(rank < TOPK_GROUPS, choice[g * per:(g + 1) * per], -jnp.inf))
    cm = jnp.concatenate(masked, axis=0)
    erow = lax.broadcasted_iota(I32, (N_EXPERTS, tm), 0).astype(F32)
    idxs, wts = [], []
    for _ in range(EXPERT_TOPK):
        mx = jnp.max(cm, axis=0, keepdims=True)
        ix = jnp.min(jnp.where(cm == mx, erow, float(N_EXPERTS)), axis=0, keepdims=True)
        hit = erow == ix
        idxs.append(ix)
        wts.append(jnp.sum(jnp.where(hit, scores, 0.0), axis=0, keepdims=True))
        cm = jnp.where(hit, -jnp.inf, cm)
    wt = jnp.concatenate(wts, axis=0)
    idx_ref[...] = jnp.concatenate(idxs, axis=0).astype(I32)
    wgt_ref[...] = wt / jnp.sum(wt, axis=0, keepdims=True) * ROUTED_SCALE


def _router(h2_flat, wr_t, bias_col):
    t, d = h2_flat.shape
    tm = _tile(t, 512)
    return pl.pallas_call(
        _router_kernel,
        grid=(t // tm,),
        in_specs=[pl.BlockSpec((tm, d), lambda i: (i, 0)),
                  pl.BlockSpec((N_EXPERTS, d), lambda i: (0, 0)),
                  pl.BlockSpec((N_EXPERTS, 1), lambda i: (0, 0))],
        out_specs=[pl.BlockSpec((EXPERT_TOPK, tm), lambda i: (0, i)),
                   pl.BlockSpec((EXPERT_TOPK, tm), lambda i: (0, i))],
        out_shape=[jax.ShapeDtypeStruct((EXPERT_TOPK, t), I32),
                   jax.ShapeDtypeStruct((EXPERT_TOPK, t), F32)],
        compiler_params=_cparams(("arbitrary",)),
        name="router_topk",
    )(h2_flat, wr_t, bias_col)


MOE_DMA_UNROLL = 8


def _moe_kernel(ech_ref, nact_ref, tokc_ref, tokn_ref, dst_ref, h_hbm, w1_ref, w3_ref, w2_ref,
                yk_hbm, xs, ys, gsem, ssem, *, dump_row0):
    del ech_ref
    c = pl.program_id(0)
    nc = pl.num_programs(0)
    nact = nact_ref[0]
    rows = MOE_CHUNK
    slot = c % 2

    def gather_copy(tok_ref, r, s):
        return pltpu.make_async_copy(h_hbm.at[pl.ds(tok_ref[0, 0, r], 1), :],
                                     xs.at[s, pl.ds(r, 1), :], gsem.at[s])

    def scatter_copy(r, dst_row):
        return pltpu.make_async_copy(ys.at[pl.ds(r, 1), :], yk_hbm.at[pl.ds(dst_row, 1), :], ssem.at[0])

    def per_row(fn):
        def body(r, carry):
            fn(r)
            return carry
        lax.fori_loop(0, rows, body, 0, unroll=MOE_DMA_UNROLL)

    def wait_scatter():
        per_row(lambda r: scatter_copy(r, 0).wait())

    @pl.when(c == 0)
    def _():
        ys[...] = jnp.zeros_like(ys)
        per_row(lambda r: scatter_copy(r, dump_row0 + r).start())
        per_row(lambda r: gather_copy(tokc_ref, r, 0).start())

    @pl.when(c < nact)
    def _():
        per_row(lambda r: gather_copy(tokn_ref, r, 1 - slot).start())
        per_row(lambda r: gather_copy(tokc_ref, r, slot).wait())

        lo, hi = _unpack_bf16_pairs(xs[slot])
        x = jnp.concatenate([lo, hi], axis=1).astype(BF16)
        u = jnp.dot(x, w1_ref[0], preferred_element_type=F32)
        v = jnp.dot(x, w3_ref[0], preferred_element_type=F32)
        hm = (u * jax.nn.sigmoid(u) * v).astype(BF16)
        d = w2_ref.shape[2]
        half = YK_PACK_TILE // 2
        tiles = []
        for j in range(d // YK_PACK_TILE):
            yj = jnp.dot(hm, w2_ref[0, :, j * YK_PACK_TILE:(j + 1) * YK_PACK_TILE],
                         preferred_element_type=F32)
            tiles.append(_pack_bf16_pairs(yj.astype(BF16)))
        wait_scatter()
        for j, tile in enumerate(tiles):
            ys[:, j * half:(j + 1) * half] = tile
        per_row(lambda r: scatter_copy(r, dst_ref[0, 0, r]).start())

    @pl.when(c == nc - 1)
    def _():
        wait_scatter()
        per_row(lambda r: gather_copy(tokc_ref, r, nact % 2).wait())


def _moe(e_chunk, n_active, tok_pad, dst_pad, h2_packed, w1, w3, w2, n_assign):
    t, dp = h2_packed.shape
    n_chunks = tok_pad.shape[0]
    d, f = w1.shape[1], w1.shape[2]
    assert d == 2 * dp and d % YK_PACK_TILE == 0
    rows = MOE_CHUNK
    smem_blk = lambda fn: pl.BlockSpec((1, 1, rows), fn, memory_space=pltpu.SMEM)
    grid_spec = pltpu.PrefetchScalarGridSpec(
        num_scalar_prefetch=2,
        grid=(n_chunks,),
        in_specs=[smem_blk(lambda c, ech, na: (c, 0, 0)),
                  smem_blk(lambda c, ech, na: (jnp.minimum(c + 1, n_chunks - 1), 0, 0)),
                  smem_blk(lambda c, ech, na: (c, 0, 0)),
                  pl.BlockSpec(memory_space=pl.ANY),
                  pl.BlockSpec((1, d, f), lambda c, ech, na: (ech[c], 0, 0)),
                  pl.BlockSpec((1, d, f), lambda c, ech, na: (ech[c], 0, 0)),
                  pl.BlockSpec((1, f, d), lambda c, ech, na: (ech[c], 0, 0))],
        out_specs=pl.BlockSpec(memory_space=pl.ANY),
        scratch_shapes=[pltpu.VMEM((2, rows, dp), U32),
                        pltpu.VMEM((rows, dp), U32),
                        pltpu.SemaphoreType.DMA((2,)),
                        pltpu.SemaphoreType.DMA((1,))],
    )
    return pl.pallas_call(
        functools.partial(_moe_kernel, dump_row0=n_assign),
        grid_spec=grid_spec,
        out_shape=jax.ShapeDtypeStruct((n_assign + rows, dp), U32),
        compiler_params=_cparams(("arbitrary",)),
        name="routed_experts",
    )(e_chunk, n_active, tok_pad, tok_pad, dst_pad, h2_packed, w1, w3, w2)


def _dispatch(idx_t):
    k, t = idx_t.shape
    e, c = N_EXPERTS, MOE_CHUNK
    m = k * t
    n_chunks = m // c + e
    e_flat = idx_t.reshape(m)
    order = jnp.argsort(e_flat).astype(I32)
    counts = jnp.sum((e_flat[None, :] == jnp.arange(e, dtype=I32)[:, None]).astype(I32), axis=1)
    starts = jnp.cumsum(counts) - counts
    padded = (counts + c - 1) // c * c
    pends = jnp.cumsum(padded)
    pstarts = pends - padded
    c_start = jnp.arange(n_chunks, dtype=I32) * c
    e_chunk = jnp.minimum(jnp.sum((c_start[:, None] >= pends[None, :]).astype(I32), axis=1), e - 1)
    r = jnp.arange(c, dtype=I32)[None, :]
    off = (c_start - pstarts[e_chunk])[:, None] + r
    valid = (off < counts[e_chunk][:, None]) & (c_start < pends[-1])[:, None]
    a_p = order[jnp.where(valid, starts[e_chunk][:, None] + off, 0)]
    tok_pad = jnp.where(valid, a_p % t, 0).astype(I32)
    dst_pad = jnp.where(valid, a_p, m + r).astype(I32)
    n_active = (pends[-1] // c).astype(I32).reshape(1)
    return e_chunk.astype(I32), n_active, tok_pad[:, None, :], dst_pad[:, None, :], m


def _alibi_slopes(n_heads):
    return 2.0 ** (-8.0 * jnp.arange(1, n_heads + 1, dtype=F32) / n_heads)


def _bf16_parts(v):
    p1 = v.astype(BF16).astype(F32)
    p2 = (v - p1).astype(BF16).astype(F32)
    p3 = (v - p1 - p2).astype(BF16).astype(F32)
    return jnp.stack([p1, p2, p3], axis=1)


def _layer(x, c_pad, ln1_gain, ln2_gain, w_ada, b_ada, w_in, moba_q_gain, moba_k_gain, dsa_q_gain,
           dsa_k_gain, idx_k_gain, w_proj_moba, w_proj_dsa, w_out, w_router, router_bias,
           w1_experts, w3_experts, w2_experts, w1_shared, w3_shared, w2_shared):
    b, s, d = x.shape
    t = b * s
    mw = MOBA_HEADS * HEAD_DIM
    qw = DSA_HEADS * HEAD_DIM
    kvw = DSA_KV_HEADS * HEAD_DIM
    iw = IDX_HEADS * IDX_DIM

    mod = _ada(c_pad, w_ada, b_ada.reshape(1, -1))[:b]
    sh1, sc1, g1, sh2, sc2, g2 = jnp.split(mod, 6, axis=-1)

    (h,) = _normmod(x, ln1_gain, sc1, sh1, with_packed=False)

    o_qa, o_ka, o_va = 0, mw, 2 * mw
    o_qb = 3 * mw
    o_kb = o_qb + qw
    o_vb = o_kb + kvw
    o_qi = o_vb + kvw
    o_ki = o_qi + iw
    o_wi = o_ki + IDX_DIM
    o_ga = o_wi + IDX_HEADS
    o_gb = o_ga + d
    cols = lambda a, n: w_in[:, a:a + n]

    scale = HEAD_DIM ** -0.5 * LOG2E
    w_qk = jnp.concatenate([cols(o_qa, mw), cols(o_ka, mw), cols(o_qb, qw), cols(o_kb, kvw)], axis=1).astype(BF16)
    g_qk = jnp.concatenate([jnp.tile(moba_q_gain * scale, MOBA_HEADS), jnp.tile(moba_k_gain, MOBA_HEADS),
                            jnp.tile(dsa_q_gain * scale, DSA_HEADS), jnp.tile(dsa_k_gain, DSA_KV_HEADS)])
    qk = _proj(h, w_qk, g_qk.reshape(1, -1), "headnorm", BF16)
    w_vq = jnp.concatenate([cols(o_va, mw), cols(o_vb, kvw), cols(o_qi, iw)], axis=1).astype(BF16)
    vq = _proj(h, w_vq, jnp.zeros((1, w_vq.shape[1]), F32), "plain", BF16)
    w_gate = cols(o_ga, 2 * d).astype(BF16)
    gates = _proj(h, w_gate, jnp.zeros((1, 2 * d), F32), "sigmoid", BF16)
    w_ix = jnp.pad(cols(o_ki, IDX_DIM + IDX_HEADS), ((0, 0), (0, LANES - IDX_DIM - IDX_HEADS))).astype(BF16)
    aux_ix = jnp.concatenate([idx_k_gain, jnp.full((IDX_HEADS,), IDX_HEADS ** -0.5 * IDX_DIM ** -0.5, F32),
                              jnp.zeros((LANES - IDX_DIM - IDX_HEADS,), F32)])
    kiw = _proj(h, w_ix, aux_ix.reshape(1, -1), "indexer", F32, tn_pref=LANES)

    hd = HEAD_DIM
    o_a = _moba(_bf16_parts(_alibi_slopes(MOBA_HEADS) * LOG2E), qk, vq, q_col0=0, k_col0=mw // hd, v_col0=0)

    qi_t = vq[:, :, mw + kvw:].reshape(b, s, IDX_HEADS, IDX_DIM).transpose(0, 2, 1, 3)
    ki = kiw[:, :, :IDX_DIM].astype(BF16)
    wi = kiw[:, :, IDX_DIM:IDX_DIM + IDX_HEADS]
    bias = _dsa_select(qi_t, ki, wi)
    o_b = _dsa_attn(_bf16_parts(_alibi_slopes(DSA_HEADS) * LOG2E), qk, vq, bias, q_col0=2 * mw // hd,
                    k_col0=(2 * mw + qw) // hd, v_col0=mw // hd)

    merged = _merge(o_a, w_proj_moba.astype(BF16), o_b, w_proj_dsa.astype(BF16), gates)
    x1 = _resid_mm(merged, w_out.astype(BF16), x, g1)

    h2, h2_packed = _normmod(x1, ln2_gain, sc2, sh2, with_packed=True)
    wr_t = w_router.T.astype(BF16)
    idx_t, wgt_t = _router(h2.reshape(t, d), wr_t, router_bias.reshape(-1, 1))
    e_chunk, n_active, tok_pad, dst_pad, n_assign = _dispatch(idx_t)
    yk = _moe(e_chunk, n_active, tok_pad, dst_pad, h2_packed.reshape(t, d // 2),
              w1_experts.astype(BF16), w3_experts.astype(BF16), w2_experts.astype(BF16), n_assign)
    hs = _swiglu_up(h2, w1_shared.astype(BF16), w3_shared.astype(BF16))
    return _final(hs, w2_shared.astype(BF16), yk, wgt_t.T, x1, g2)


def kernel(x, c, ln1_gain, ln2_gain, w_ada, b_ada, w_in, moba_q_gain, moba_k_gain, dsa_q_gain, dsa_k_gain, idx_k_gain, w_proj_moba, w_proj_dsa, w_out, w_router, router_bias, w1_experts, w3_experts, w2_experts, w1_shared, w3_shared, w2_shared):
    depth = w_ada.shape[0]
    b = x.shape[0]
    c_pad = jnp.pad(c, ((0, (-b) % 8), (0, 0)))
    params = (ln1_gain, ln2_gain, w_ada, b_ada, w_in, moba_q_gain, moba_k_gain, dsa_q_gain, dsa_k_gain,
              idx_k_gain, w_proj_moba, w_proj_dsa, w_out, w_router, router_bias, w1_experts, w3_experts,
              w2_experts, w1_shared, w3_shared, w2_shared)
    for l in range(depth):
        x = _layer(x, c_pad, *[p[l] for p in params])
    return x
```

```python
import functools

import jax
import jax.numpy as jnp
from jax import lax
from jax.experimental import pallas as pl
from jax.experimental.pallas import tpu as pltpu

F32 = jnp.float32
BF16 = jnp.bfloat16
I32 = jnp.int32
U32 = jnp.uint32

HEAD_DIM = 128
MOBA_HEADS = 16
MOBA_BLOCK = 256
MOBA_TOPK = 3
DSA_HEADS = 16
DSA_KV_HEADS = 4
DSA_TOPK_MAX = 256
IDX_HEADS = 32
IDX_DIM = 64
N_EXPERTS = 64
EXPERT_TOPK = 8
N_GROUPS = 8
TOPK_GROUPS = 4
ROUTED_SCALE = 2.5
NORM_EPS = 1e-6

LANES = 128
V7X_VMEM_LIMIT_BYTES = 56 * 1024 * 1024

MOE_CHUNK = 256
YK_PACK_TILE = 512
NEG = -1e30
INT_MIN = -2 ** 31
LOG2E = 1.4426950408889634

_NT = (((1,), (1,)), ((), ()))


def _cparams(sem):
    return pltpu.CompilerParams(dimension_semantics=sem, vmem_limit_bytes=V7X_VMEM_LIMIT_BYTES)


def _tile(n, pref):
    if n <= pref:
        return n
    t = pref - pref % LANES
    while t > LANES and n % t:
        t -= LANES
    assert n % t == 0, (n, pref)
    return t


def _ada_kernel(c_ref, w_ref, b_ref, o_ref):
    c = c_ref[...]
    s = c * jax.nn.sigmoid(c)
    acc = jnp.dot(s.astype(BF16), w_ref[...].astype(BF16), preferred_element_type=F32)
    o_ref[...] = acc + b_ref[...]


def _ada(c_pad, w_ada, b_ada):
    rows, d = c_pad.shape
    n = w_ada.shape[1]
    tn = _tile(n, 512)
    return pl.pallas_call(
        _ada_kernel,
        grid=(n // tn,),
        in_specs=[pl.BlockSpec((rows, d), lambda j: (0, 0)),
                  pl.BlockSpec((d, tn), lambda j: (0, j)),
                  pl.BlockSpec((1, tn), lambda j: (0, j))],
        out_specs=pl.BlockSpec((rows, tn), lambda j: (0, j)),
        out_shape=jax.ShapeDtypeStruct((rows, n), F32),
        compiler_params=_cparams(("arbitrary",)),
        name="ada_mod",
    )(c_pad, w_ada, b_ada)


def _pack_bf16_pairs(v):
    n = v.shape[1] // 2
    bits = lax.bitcast_convert_type(v.astype(F32), U32)
    return (bits[:, :n] >> 16) | bits[:, n:]


def _unpack_bf16_pairs(w):
    lo = lax.bitcast_convert_type(w << 16, F32)
    hi = lax.bitcast_convert_type(w & jnp.uint32(0xFFFF0000), F32)
    return lo, hi


def _normmod_kernel(x_ref, gain_ref, sc_ref, sh_ref, *o_refs):
    x = x_ref[0]
    y = x * lax.rsqrt(jnp.mean(x * x, axis=-1, keepdims=True) + NORM_EPS) * gain_ref[...]
    h = (y * (1.0 + sc_ref[0]) + sh_ref[0]).astype(BF16)
    o_refs[0][0] = h
    if len(o_refs) > 1:
        o_refs[1][0] = _pack_bf16_pairs(h)


def _normmod(x, gain, sc, sh, with_packed):
    b, s, d = x.shape
    ts = _tile(s, 256)
    blk = pl.BlockSpec((1, ts, d), lambda bi, i: (bi, i, 0))
    vec = pl.BlockSpec((1, 1, d), lambda bi, i: (bi, 0, 0))
    out_shape = [jax.ShapeDtypeStruct((b, s, d), BF16)]
    out_specs = [blk]
    if with_packed:
        out_shape.append(jax.ShapeDtypeStruct((b, s, d // 2), U32))
        out_specs.append(pl.BlockSpec((1, ts, d // 2), lambda bi, i: (bi, i, 0)))
    return pl.pallas_call(
        _normmod_kernel,
        grid=(b, s // ts),
        in_specs=[blk, pl.BlockSpec((1, d), lambda bi, i: (0, 0)), vec, vec],
        out_specs=out_specs,
        out_shape=out_shape,
        compiler_params=_cparams(("arbitrary", "arbitrary")),
        name="norm_modulate",
    )(x, gain.reshape(1, d), sc.reshape(b, 1, d), sh.reshape(b, 1, d))


def _proj_kernel(a_ref, w_ref, aux_ref, o_ref, *, mode):
    acc = jnp.dot(a_ref[0], w_ref[...], preferred_element_type=F32)
    tn = acc.shape[1]
    if mode == "plain":
        o_ref[0] = acc.astype(o_ref.dtype)
    elif mode == "sigmoid":
        o_ref[0] = jax.nn.sigmoid(acc).astype(o_ref.dtype)
    elif mode == "headnorm":
        for g in range(tn // HEAD_DIM):
            sl = slice(g * HEAD_DIM, (g + 1) * HEAD_DIM)
            blk = acc[:, sl]
            ms = jnp.mean(blk * blk, axis=-1, keepdims=True)
            o_ref[0, :, sl] = (blk * lax.rsqrt(ms + NORM_EPS) * aux_ref[:, sl]).astype(o_ref.dtype)
    elif mode == "indexer":
        lane = lax.broadcasted_iota(I32, acc.shape, 1)
        is_key = lane < IDX_DIM
        ms = jnp.sum(jnp.where(is_key, acc * acc, 0.0), axis=-1, keepdims=True) / IDX_DIM
        o_ref[0] = acc * jnp.where(is_key, lax.rsqrt(ms + NORM_EPS), 1.0) * aux_ref[...]
    else:
        raise ValueError(mode)


def _proj(a, w, aux, mode, out_dtype, tm_pref=1024, tn_pref=512):
    b, s, k = a.shape
    n = w.shape[1]
    tm, tn = _tile(s, tm_pref), _tile(n, tn_pref)
    return pl.pallas_call(
        functools.partial(_proj_kernel, mode=mode),
        grid=(b, s // tm, n // tn),
        in_specs=[pl.BlockSpec((1, tm, k), lambda bi, i, j: (bi, i, 0)),
                  pl.BlockSpec((k, tn), lambda bi, i, j: (0, j)),
                  pl.BlockSpec((1, tn), lambda bi, i, j: (0, j))],
        out_specs=pl.BlockSpec((1, tm, tn), lambda bi, i, j: (bi, i, j)),
        out_shape=jax.ShapeDtypeStruct((b, s, n), out_dtype),
        compiler_params=_cparams(("arbitrary", "arbitrary", "arbitrary")),
        name="proj_" + mode,
    )(a, w, aux)


AUG_ALIBI = 0
AUG_SEL = 8
MOBA_QBLOCKS = 2


def _moba_kernel(sp_ref, q_ref, k_ref, v_ref, o_ref, kaug_sc, vt_sc, kmean_sc, *, nb, gt, qb):
    h = pl.program_id(1)
    ip = pl.program_id(2)
    blk = MOBA_BLOCK
    tq = qb * blk
    nbp = kmean_sc.shape[0]
    gblocks = gt // blk

    @pl.when(ip == 0)
    def _():
        kmean_sc[...] = jnp.zeros_like(kmean_sc)
        lane = lax.broadcasted_iota(I32, (blk, LANES), 1)
        row = lax.broadcasted_iota(I32, (blk, LANES), 0)
        for n in range(nb):
            rows = slice(n * blk, (n + 1) * blk)
            kb = k_ref[0, rows, :]
            kmean_sc[n:n + 1, :] = jnp.mean(kb.astype(F32), axis=0, keepdims=True)
            aug = jnp.where(lane == AUG_SEL + n, 1.0, _alibi_key_aug(n * blk + row, lane))
            kaug_sc[rows, :HEAD_DIM] = kb
            kaug_sc[rows, HEAD_DIM:] = aug.astype(BF16)
            vt_sc[:, rows] = v_ref[0, rows, :].astype(F32).T.astype(BF16)

    q = q_ref[0]
    gate_t = lax.dot_general(kmean_sc[...], q.astype(F32), _NT, preferred_element_type=F32,
                             precision=lax.Precision.HIGHEST)
    r_iota = lax.broadcasted_iota(I32, gate_t.shape, 0)
    assert blk & (blk - 1) == 0
    own = ip * qb + (lax.broadcasted_iota(I32, gate_t.shape, 1) >> (blk.bit_length() - 1))
    rank = jnp.zeros(gate_t.shape, I32)
    for m in range(nb):
        gm = gate_t[m:m + 1, :]
        beats = ((gm > gate_t) | ((gm == gate_t) & (m < r_iota))) & (m < own)
        rank = rank + jnp.where(beats, 1, 0)
    keep_t = ((r_iota < own) & (rank < MOBA_TOPK)) | (r_iota == own)
    selb_t = jnp.where(keep_t, 0.0, NEG)

    augq_t = jnp.concatenate([_alibi_query_aug(sp_ref, h, (8, tq), 0), selb_t,
                              jnp.zeros((LANES - 8 - nbp, tq), F32)], axis=0)
    qa = jnp.concatenate([q, augq_t.T.astype(BF16)], axis=1)

    t_pos = ip * tq + lax.broadcasted_iota(I32, (gt, tq), 1)
    s_off = lax.broadcasted_iota(I32, (gt, tq), 0)

    def group(gi, carry, causal):
        m_prev, l_prev, acc_prev = carry
        g0 = pl.multiple_of(gi * gt, gt)
        st = lax.dot_general(kaug_sc[pl.ds(g0, gt), :], qa, _NT, preferred_element_type=F32)
        if causal:
            st = jnp.where(g0 + s_off <= t_pos, st, NEG)
        m_new = jnp.maximum(m_prev, jnp.max(st, axis=0, keepdims=True))
        alpha = jnp.exp2(m_prev - m_new)
        p = jnp.exp2(st - m_new)
        l_new = alpha * l_prev + jnp.sum(p, axis=0, keepdims=True)
        pv = jnp.dot(vt_sc[:, pl.ds(g0, gt)], p.astype(BF16), preferred_element_type=F32)
        return m_new, l_new, alpha * acc_prev + pv

    init = (jnp.full((1, tq), NEG, F32), jnp.zeros((1, tq), F32), jnp.zeros((HEAD_DIM, tq), F32))
    n_full = (ip * qb) // gblocks
    carry = lax.fori_loop(0, n_full, lambda gi, c: group(gi, c, False), init)
    _, l_fin, acc_fin = group(n_full, carry, True)
    o_ref[0] = (acc_fin / l_fin).T.astype(o_ref.dtype)


def _moba(slope_parts, qk, vq, q_col0, k_col0, v_col0):
    b, s, _ = qk.shape
    assert s % MOBA_BLOCK == 0 and s <= 4096
    nb = s // MOBA_BLOCK
    nbp = -(-nb // 8) * 8
    assert AUG_SEL + nbp <= LANES
    gt = min(s, 4 * MOBA_BLOCK)
    qb = MOBA_QBLOCKS
    assert s % gt == 0 and nb % qb == 0 and (gt // MOBA_BLOCK) % qb == 0
    tq = qb * MOBA_BLOCK
    return pl.pallas_call(
        functools.partial(_moba_kernel, nb=nb, gt=gt, qb=qb),
        grid=(b, MOBA_HEADS, nb // qb),
        in_specs=[pl.BlockSpec(memory_space=pltpu.SMEM),
                  pl.BlockSpec((1, tq, HEAD_DIM), lambda bi, h, i: (bi, i, q_col0 + h)),
                  pl.BlockSpec((1, s, HEAD_DIM), lambda bi, h, i: (bi, 0, k_col0 + h)),
                  pl.BlockSpec((1, s, HEAD_DIM), lambda bi, h, i: (bi, 0, v_col0 + h))],
        out_specs=pl.BlockSpec((1, tq, HEAD_DIM), lambda bi, h, i: (bi, i, h)),
        out_shape=jax.ShapeDtypeStruct((b, s, MOBA_HEADS * HEAD_DIM), BF16),
        scratch_shapes=[pltpu.VMEM((s, 2 * HEAD_DIM), BF16),
                        pltpu.VMEM((HEAD_DIM, s), BF16),
                        pltpu.VMEM((nbp, HEAD_DIM), F32)],
        compiler_params=_cparams(("arbitrary", "arbitrary", "arbitrary")),
        name="moba_attention",
    )(slope_parts, qk, qk, vq)


def _dsa_select_kernel(qi_ref, ki_ref, wi_ref, bias_ref, key_sc, qs_sc, *, tq, tk, nsel, hgroup):
    j = pl.program_id(1)
    t0 = j * tq
    nkc = (t0 + tq + tk - 1) // tk
    n_hg = IDX_HEADS // hgroup

    bias_ref[...] = jnp.full(bias_ref.shape, NEG, bias_ref.dtype)
    for hd in range(IDX_HEADS):
        qs_sc[hd * tq:(hd + 1) * tq, :] = qi_ref[0, :, hd * IDX_DIM:(hd + 1) * IDX_DIM]
    w = wi_ref[0]
    row_t = t0 + lax.broadcasted_iota(I32, (tq, tk), 0)
    col_i = lax.broadcasted_iota(I32, (tq, tk), 1)

    def score_chunk(c, _):
        c0 = pl.multiple_of(c * tk, tk)
        kc = ki_ref[0, pl.ds(c0, tk), :]
        acc = jnp.zeros((tq, tk), F32)
        for g in range(n_hg):
            qg = qs_sc[g * hgroup * tq:(g + 1) * hgroup * tq, :]
            d = lax.dot_general(qg, kc, _NT, preferred_element_type=F32)
            for hh in range(hgroup):
                hd = g * hgroup + hh
                acc = acc + w[:, hd:hd + 1] * jnp.maximum(d[hh * tq:(hh + 1) * tq], 0.0)
        bits = lax.bitcast_convert_type(acc, I32)
        key = jnp.where(bits >= 0, bits, bits ^ 0x7FFFFFFF)
        key = jnp.where(c0 + col_i <= row_t, key, INT_MIN)
        key_sc[:, pl.ds(c0, tk)] = key
        return 0

    lax.fori_loop(0, nkc, score_chunk, 0)

    def bit_body(bi, prefix):
        cand = prefix + jnp.left_shift(jnp.int32(1), 31 - bi)

        def cnt_chunk(c, cnt):
            blk = key_sc[:, pl.ds(pl.multiple_of(c * tk, tk), tk)]
            ge = jnp.where(blk >= cand, 1.0, 0.0)
            for g in range(tk // LANES):
                cnt = cnt + ge[:, g * LANES:(g + 1) * LANES]
            return cnt

        cnt = lax.fori_loop(0, nkc, cnt_chunk, jnp.zeros((tq, LANES), F32))
        total = jnp.sum(cnt, axis=-1, keepdims=True)
        return jnp.where(total >= nsel, cand, prefix)

    thr = lax.fori_loop(0, 32, bit_body, jnp.full((tq, 1), INT_MIN, I32))

    def out_chunk(c, _):
        c0 = pl.multiple_of(c * tk, tk)
        blk = key_sc[:, pl.ds(c0, tk)]
        keep = (blk >= thr) & (c0 + col_i <= row_t)
        bias_ref[0, :, pl.ds(c0, tk)] = jnp.where(keep, 0.0, NEG).astype(bias_ref.dtype)
        return 0

    lax.fori_loop(0, nkc, out_chunk, 0)


def _dsa_select(vq, qi_blk, ki, wi):
    b, s, _ = vq.shape
    iw = IDX_HEADS * IDX_DIM
    tq = _tile(s, 128)
    tk = _tile(s, 512)
    nsel = min(DSA_TOPK_MAX, s // 4)
    return pl.pallas_call(
        functools.partial(_dsa_select_kernel, tq=tq, tk=tk, nsel=nsel, hgroup=8),
        grid=(b, s // tq),
        in_specs=[pl.BlockSpec((1, tq, iw), lambda bi, j: (bi, j, qi_blk)),
                  pl.BlockSpec((1, s, IDX_DIM), lambda bi, j: (bi, 0, 0)),
                  pl.BlockSpec((1, tq, IDX_HEADS), lambda bi, j: (bi, j, 0))],
        out_specs=pl.BlockSpec((1, tq, s), lambda bi, j: (bi, j, 0)),
        out_shape=jax.ShapeDtypeStruct((b, s, s), BF16),
        scratch_shapes=[pltpu.VMEM((tq, s), I32), pltpu.VMEM((IDX_HEADS * tq, IDX_DIM), BF16)],
        compiler_params=_cparams(("arbitrary", "arbitrary")),
        name="dsa_select",
    )(vq, ki, wi)


def _alibi_key_aug(pos, lane):
    alibi = jnp.where((lane & 1) == 0, ((pos >> 4) << 4).astype(F32), (pos & 15).astype(F32))
    return jnp.where(lane < AUG_ALIBI + 6, alibi, 0.0)


def _alibi_query_aug(sp_ref, h, shape, axis):
    ix = lax.broadcasted_iota(I32, shape, axis)
    return jnp.where(ix < 2, sp_ref[h, 0], jnp.where(ix < 4, sp_ref[h, 1], jnp.where(ix < 6, sp_ref[h, 2], 0.0)))


def _dsa_attn_kernel(sp_ref, q_ref, k_ref, v_ref, bias_ref, o_ref, kaug_sc, *, tq, tk, grp):
    g = pl.program_id(1)
    j = pl.program_id(2)
    t0 = j * tq
    nk = (t0 + tq + tk - 1) // tk
    rows = grp * tq
    dh = HEAD_DIM
    s_len = kaug_sc.shape[0]

    @pl.when(j == 0)
    def _():
        lane = lax.broadcasted_iota(I32, (tk, LANES), 1)
        row = lax.broadcasted_iota(I32, (tk, LANES), 0)
        for n in range(s_len // tk):
            rs = slice(n * tk, (n + 1) * tk)
            kaug_sc[rs, :dh] = k_ref[0, rs, :]
            kaug_sc[rs, dh:] = _alibi_key_aug(n * tk + row, lane).astype(BF16)

    q = jnp.concatenate(
        [jnp.concatenate([q_ref[0, :, r * dh:(r + 1) * dh],
                          _alibi_query_aug(sp_ref, g * grp + r, (tq, LANES), 1).astype(BF16)], axis=1)
         for r in range(grp)], axis=0)

    def body(c, carry):
        m_prev, l_prev, acc_prev = carry
        c0 = pl.multiple_of(c * tk, tk)
        s = lax.dot_general(q, kaug_sc[pl.ds(c0, tk), :], _NT, preferred_element_type=F32)
        mb = bias_ref[0, :, pl.ds(c0, tk)].astype(F32)
        lg = jnp.concatenate([s[r * tq:(r + 1) * tq] + mb for r in range(grp)], axis=0)
        m_new = jnp.maximum(m_prev, jnp.max(lg, axis=-1, keepdims=True))
        alpha = jnp.exp2(m_prev - m_new)
        p = jnp.exp2(lg - m_new)
        l_new = alpha * l_prev + jnp.sum(p, axis=-1, keepdims=True)
        pv = jnp.dot(p.astype(BF16), v_ref[0, pl.ds(c0, tk), :], preferred_element_type=F32)
        return m_new, l_new, alpha * acc_prev + pv

    init = (jnp.full((rows, 1), NEG, F32), jnp.zeros((rows, 1), F32), jnp.zeros((rows, dh), F32))
    _, l_fin, acc_fin = lax.fori_loop(0, nk, body, init)
    out = acc_fin / l_fin
    for r in range(grp):
        o_ref[0, :, r * dh:(r + 1) * dh] = out[r * tq:(r + 1) * tq].astype(o_ref.dtype)


def _dsa_attn(slope_parts, qk, vq, bias, q_col0, k_col0, v_col0):
    b, s, _ = qk.shape
    assert s <= 4096
    grp = DSA_HEADS // DSA_KV_HEADS
    gw = grp * HEAD_DIM
    assert (q_col0 * HEAD_DIM) % gw == 0
    q_blk0 = q_col0 * HEAD_DIM // gw
    tq = _tile(s, 128)
    tk = _tile(s, 512)
    return pl.pallas_call(
        functools.partial(_dsa_attn_kernel, tq=tq, tk=tk, grp=grp),
        grid=(b, DSA_KV_HEADS, s // tq),
        in_specs=[pl.BlockSpec(memory_space=pltpu.SMEM),
                  pl.BlockSpec((1, tq, gw), lambda bi, g, j: (bi, j, q_blk0 + g)),
                  pl.BlockSpec((1, s, HEAD_DIM), lambda bi, g, j: (bi, 0, k_col0 + g)),
                  pl.BlockSpec((1, s, HEAD_DIM), lambda bi, g, j: (bi, 0, v_col0 + g)),
                  pl.BlockSpec((1, tq, s), lambda bi, g, j: (bi, j, 0))],
        out_specs=pl.BlockSpec((1, tq, gw), lambda bi, g, j: (bi, j, g)),
        out_shape=jax.ShapeDtypeStruct((b, s, DSA_HEADS * HEAD_DIM), BF16),
        scratch_shapes=[pltpu.VMEM((s, 2 * HEAD_DIM), BF16)],
        compiler_params=_cparams(("arbitrary", "arbitrary", "arbitrary")),
        name="dsa_attention",
    )(slope_parts, qk, qk, vq, bias)


def _merge_kernel(oa_ref, wa_ref, ob_ref, wb_ref, ga_ref, gb_ref, o_ref):
    ya = jnp.dot(oa_ref[0], wa_ref[...], preferred_element_type=F32)
    yb = jnp.dot(ob_ref[0], wb_ref[...], preferred_element_type=F32)
    o_ref[0] = (ga_ref[0].astype(F32) * ya + gb_ref[0].astype(F32) * yb).astype(o_ref.dtype)


def _merge(oa, wa, ob, wb, gates):
    b, s, ka = oa.shape
    kb = ob.shape[2]
    n = wa.shape[1]
    tm, tn = _tile(s, 1024), _tile(n, 512)
    nj = n // tn
    return pl.pallas_call(
        _merge_kernel,
        grid=(b, s // tm, nj),
        in_specs=[pl.BlockSpec((1, tm, ka), lambda bi, i, j: (bi, i, 0)),
                  pl.BlockSpec((ka, tn), lambda bi, i, j: (0, j)),
                  pl.BlockSpec((1, tm, kb), lambda bi, i, j: (bi, i, 0)),
                  pl.BlockSpec((kb, tn), lambda bi, i, j: (0, j)),
                  pl.BlockSpec((1, tm, tn), lambda bi, i, j: (bi, i, j)),
                  pl.BlockSpec((1, tm, tn), lambda bi, i, j: (bi, i, nj + j))],
        out_specs=pl.BlockSpec((1, tm, tn), lambda bi, i, j: (bi, i, j)),
        out_shape=jax.ShapeDtypeStruct((b, s, n), BF16),
        compiler_params=_cparams(("arbitrary", "arbitrary", "arbitrary")),
        name="gated_merge",
    )(oa, wa, ob, wb, gates, gates)


def _resid_kernel(a_ref, w_ref, x_ref, g_ref, o_ref):
    y = jnp.dot(a_ref[0], w_ref[...], preferred_element_type=F32)
    o_ref[0] = x_ref[0] + g_ref[0] * y


def _resid_mm(a, w, x, g):
    b, s, k = a.shape
    n = w.shape[1]
    tm, tn = _tile(s, 1024), _tile(n, 512)
    return pl.pallas_call(
        _resid_kernel,
        grid=(b, s // tm, n // tn),
        in_specs=[pl.BlockSpec((1, tm, k), lambda bi, i, j: (bi, i, 0)),
                  pl.BlockSpec((k, tn), lambda bi, i, j: (0, j)),
                  pl.BlockSpec((1, tm, tn), lambda bi, i, j: (bi, i, j)),
                  pl.BlockSpec((1, 1, tn), lambda bi, i, j: (bi, 0, j))],
        out_specs=pl.BlockSpec((1, tm, tn), lambda bi, i, j: (bi, i, j)),
        out_shape=jax.ShapeDtypeStruct((b, s, n), F32),
        compiler_params=_cparams(("arbitrary", "arbitrary", "arbitrary")),
        name="out_proj_residual",
    )(a, w, x, g.reshape(b, 1, n))


def _swiglu_kernel(a_ref, w1_ref, w3_ref, o_ref):
    a = a_ref[0]
    u = jnp.dot(a, w1_ref[...], preferred_element_type=F32)
    v = jnp.dot(a, w3_ref[...], preferred_element_type=F32)
    o_ref[0] = (u * jax.nn.sigmoid(u) * v).astype(o_ref.dtype)


def _swiglu_up(a, w1, w3):
    b, s, k = a.shape
    n = w1.shape[1]
    tm, tn = _tile(s, 1024), _tile(n, 512)
    return pl.pallas_call(
        _swiglu_kernel,
        grid=(b, s // tm, n // tn),
        in_specs=[pl.BlockSpec((1, tm, k), lambda bi, i, j: (bi, i, 0)),
                  pl.BlockSpec((k, tn), lambda bi, i, j: (0, j)),
                  pl.BlockSpec((k, tn), lambda bi, i, j: (0, j))],
        out_specs=pl.BlockSpec((1, tm, tn), lambda bi, i, j: (bi, i, j)),
        out_shape=jax.ShapeDtypeStruct((b, s, n), BF16),
        compiler_params=_cparams(("arbitrary", "arbitrary", "arbitrary")),
        name="shared_swiglu_up",
    )(a, w1, w3)


def _final_kernel(a_ref, w_ref, *rest):
    yk_refs, (rw_ref, x_ref, g_ref, o_ref) = rest[:EXPERT_TOPK], rest[EXPERT_TOPK:]
    tm = a_ref.shape[1]
    y = jnp.dot(a_ref[0], w_ref[...], preferred_element_type=F32)
    wk = [rw_ref[:, kk:kk + 1] for kk in range(EXPERT_TOPK)]
    los, his = [], []
    for a in range(TOK_SLAB):
        lo = hi = None
        for kk, r in enumerate(yk_refs):
            lo_r, hi_r = _unpack_bf16_pairs(r[pl.ds(a, tm, stride=TOK_SLAB), :])
            lo = lo_r * wk[kk] if lo is None else lo + lo_r * wk[kk]
            hi = hi_r * wk[kk] if hi is None else hi + hi_r * wk[kk]
        los.append(lo)
        his.append(hi)
    o_ref[0] = x_ref[0] + g_ref[0] * (y + jnp.concatenate(los + his, axis=1))


def _final(hs, w2, yk, route_w, x, g):
    b, s, k = hs.shape
    n = w2.shape[1]
    assert n == 2 * TOK_SLAB * LANES
    tm = _tile(s, 128)
    nti = s // tm
    t_blocks = b * nti
    yk_specs = [pl.BlockSpec((tm * TOK_SLAB, LANES), lambda bi, i, kk=kk: (kk * t_blocks + bi * nti + i, 0))
                for kk in range(EXPERT_TOPK)]
    return pl.pallas_call(
        _final_kernel,
        grid=(b, nti),
        in_specs=[pl.BlockSpec((1, tm, k), lambda bi, i: (bi, i, 0)),
                  pl.BlockSpec((k, n), lambda bi, i: (0, 0))] + yk_specs + [
                  pl.BlockSpec((tm, EXPERT_TOPK), lambda bi, i: (bi * nti + i, 0)),
                  pl.BlockSpec((1, tm, n), lambda bi, i: (bi, i, 0)),
                  pl.BlockSpec((1, 1, n), lambda bi, i: (bi, 0, 0))],
        out_specs=pl.BlockSpec((1, tm, n), lambda bi, i: (bi, i, 0)),
        out_shape=jax.ShapeDtypeStruct((b, s, n), F32),
        compiler_params=_cparams(("arbitrary", "arbitrary")),
        name="moe_combine_residual",
    )(hs, w2, *([yk] * EXPERT_TOPK), route_w, x, g.reshape(b, 1, n))


def _router_kernel(h_ref, wr_ref, b_ref, idx_ref, wgt_ref):
    tm = h_ref.shape[0]
    per = N_EXPERTS // N_GROUPS
    logits = lax.dot_general(wr_ref[...], h_ref[...], _NT, preferred_element_type=F32)
    scores = jax.nn.sigmoid(logits)
    choice = scores + b_ref[...]
    sub = lax.broadcasted_iota(I32, (per, tm), 0).astype(F32)
    gs = []
    for g in range(N_GROUPS):
        cg = choice[g * per:(g + 1) * per]
        m1 = jnp.max(cg, axis=0, keepdims=True)
        i1 = jnp.min(jnp.where(cg == m1, sub, float(per)), axis=0, keepdims=True)
        m2 = jnp.max(jnp.where(sub == i1, -jnp.inf, cg), axis=0, keepdims=True)
        gs.append(m1 + m2)
    masked = []
    for g in range(N_GROUPS):
        rank = jnp.zeros((1, tm), F32)
        for g2 in range(N_GROUPS):
            if g2 == g:
                continue
            beats = (gs[g2] > gs[g]) | ((gs[g2] == gs[g]) & (g2 < g))
            rank = rank + jnp.where(beats, 1.0, 0.0)
        masked.append(jnp.where(rank < TOPK_GROUPS, choice[g * per:(g + 1) * per], -jnp.inf))
    cm = jnp.concatenate(masked, axis=0)
    erow = lax.broadcasted_iota(I32, (N_EXPERTS, tm), 0).astype(F32)
    idxs, wts = [], []
    for _ in range(EXPERT_TOPK):
        mx = jnp.max(cm, axis=0, keepdims=True)
        ix = jnp.min(jnp.where(cm == mx, erow, float(N_EXPERTS)), axis=0, keepdims=True)
        hit = erow == ix
        idxs.append(ix)
        wts.append(jnp.sum(jnp.where(hit, scores, 0.0), axis=0, keepdims=True))
        cm = jnp.where(hit, -jnp.inf, cm)
    wt = jnp.concatenate(wts, axis=0)
    idx_ref[...] = jnp.concatenate(idxs, axis=0).astype(I32)
    wgt_ref[...] = wt / jnp.sum(wt, axis=0, keepdims=True) * ROUTED_SCALE


def _router(h2_flat, wr_t, bias_col):
    t, d = h2_flat.shape
    tm = _tile(t, 512)
    return pl.pallas_call(
        _router_kernel,
        grid=(t // tm,),
        in_specs=[pl.BlockSpec((tm, d), lambda i: (i, 0)),
                  pl.BlockSpec((N_EXPERTS, d), lambda i: (0, 0)),
                  pl.BlockSpec((N_EXPERTS, 1), lambda i: (0, 0))],
        out_specs=[pl.BlockSpec((EXPERT_TOPK, tm), lambda i: (0, i)),
                   pl.BlockSpec((EXPERT_TOPK, tm), lambda i: (0, i))],
        out_shape=[jax.ShapeDtypeStruct((EXPERT_TOPK, t), I32),
                   jax.ShapeDtypeStruct((EXPERT_TOPK, t), F32)],
        compiler_params=_cparams(("arbitrary",)),
        name="router_topk",
    )(h2_flat, wr_t, bias_col)


MOE_DMA_UNROLL = 8
TOK_SLAB = 16


def _moe_kernel(ech_ref, nact_ref, tokc_ref, tokn_ref, dst_ref, h_hbm, w1_ref, w3_ref, w2_ref,
                yk_hbm, xs, ys, gsem, ssem, *, dump_row0):
    del ech_ref
    c = pl.program_id(0)
    nc = pl.num_programs(0)
    nact = nact_ref[0]
    rows = MOE_CHUNK
    sl = TOK_SLAB
    slot = c % 2

    def slab(ref, start):
        return ref.at[pl.ds(pl.multiple_of(start, sl), sl), :]

    def gather_copy(tok_ref, r, s):
        return pltpu.make_async_copy(slab(h_hbm, tok_ref[0, 0, r]), slab(xs.at[s], r * sl), gsem.at[s])

    def scatter_copy(r, dst_row):
        return pltpu.make_async_copy(slab(ys, r * sl), slab(yk_hbm, dst_row), ssem.at[0])

    def per_row(fn):
        def body(r, carry):
            fn(r)
            return carry
        lax.fori_loop(0, rows, body, 0, unroll=MOE_DMA_UNROLL)

    def wait_scatter():
        per_row(lambda r: scatter_copy(r, 0).wait())

    @pl.when(c == 0)
    def _():
        ys[...] = jnp.zeros_like(ys)
        per_row(lambda r: scatter_copy(r, (dump_row0 + r) * sl).start())
        per_row(lambda r: gather_copy(tokc_ref, r, 0).start())

    @pl.when(c < nact)
    def _():
        per_row(lambda r: gather_copy(tokn_ref, r, 1 - slot).start())
        per_row(lambda r: gather_copy(tokc_ref, r, slot).wait())

        xsl = xs.at[slot]
        parts = [_unpack_bf16_pairs(xsl[pl.ds(a, rows, stride=sl), :]) for a in range(sl)]
        x = jnp.concatenate([p[0] for p in parts] + [p[1] for p in parts], axis=1).astype(BF16)
        u = jnp.dot(x, w1_ref[0], preferred_element_type=F32)
        v = jnp.dot(x, w3_ref[0], preferred_element_type=F32)
        hm = (u * jax.nn.sigmoid(u) * v).astype(BF16)
        d = w2_ref.shape[2]
        nt = d // YK_PACK_TILE
        y = [jnp.dot(hm, w2_ref[0, :, j * YK_PACK_TILE:(j + 1) * YK_PACK_TILE],
                     preferred_element_type=F32).astype(BF16) for j in range(nt)]
        words = [_pack_bf16_pairs(jnp.concatenate([y[j], y[j + nt // 2]], axis=1)) for j in range(nt // 2)]
        wait_scatter()
        per_tile = YK_PACK_TILE // LANES
        for j, wj in enumerate(words):
            for q in range(per_tile):
                ys[pl.ds(j * per_tile + q, rows, stride=sl), :] = wj[:, q * LANES:(q + 1) * LANES]
        per_row(lambda r: scatter_copy(r, dst_ref[0, 0, r]).start())

    @pl.when(c == nc - 1)
    def _():
        wait_scatter()
        per_row(lambda r: gather_copy(tokc_ref, r, nact % 2).wait())


def _moe(e_chunk, n_active, tok_pad, dst_pad, h2_slabs, w1, w3, w2, n_assign):
    n_chunks = tok_pad.shape[0]
    d, f = w1.shape[1], w1.shape[2]
    assert d == 2 * TOK_SLAB * LANES and d % (2 * YK_PACK_TILE) == 0
    rows = MOE_CHUNK
    smem_blk = lambda fn: pl.BlockSpec((1, 1, rows), fn, memory_space=pltpu.SMEM)
    grid_spec = pltpu.PrefetchScalarGridSpec(
        num_scalar_prefetch=2,
        grid=(n_chunks,),
        in_specs=[smem_blk(lambda c, ech, na: (c, 0, 0)),
                  smem_blk(lambda c, ech, na: (jnp.minimum(c + 1, n_chunks - 1), 0, 0)),
                  smem_blk(lambda c, ech, na: (c, 0, 0)),
                  pl.BlockSpec(memory_space=pl.ANY),
                  pl.BlockSpec((1, d, f), lambda c, ech, na: (ech[c], 0, 0)),
                  pl.BlockSpec((1, d, f), lambda c, ech, na: (ech[c], 0, 0)),
                  pl.BlockSpec((1, f, d), lambda c, ech, na: (ech[c], 0, 0))],
        out_specs=pl.BlockSpec(memory_space=pl.ANY),
        scratch_shapes=[pltpu.VMEM((2, rows * TOK_SLAB, LANES), U32),
                        pltpu.VMEM((rows * TOK_SLAB, LANES), U32),
                        pltpu.SemaphoreType.DMA((2,)),
                        pltpu.SemaphoreType.DMA((1,))],
    )
    return pl.pallas_call(
        functools.partial(_moe_kernel, dump_row0=n_assign),
        grid_spec=grid_spec,
        out_shape=jax.ShapeDtypeStruct(((n_assign + rows) * TOK_SLAB, LANES), U32),
        compiler_params=_cparams(("arbitrary",)),
        name="routed_experts",
    )(e_chunk, n_active, tok_pad, tok_pad, dst_pad, h2_slabs, w1, w3, w2)


def _dispatch(idx_t):
    k, t = idx_t.shape
    e, c = N_EXPERTS, MOE_CHUNK
    m = k * t
    n_chunks = m // c + e
    e_flat = idx_t.reshape(m)
    order = jnp.argsort(e_flat).astype(I32)
    counts = jnp.sum((e_flat[None, :] == jnp.arange(e, dtype=I32)[:, None]).astype(I32), axis=1)
    starts = jnp.cumsum(counts) - counts
    padded = (counts + c - 1) // c * c
    pends = jnp.cumsum(padded)
    pstarts = pends - padded
    c_start = jnp.arange(n_chunks, dtype=I32) * c
    e_chunk = jnp.minimum(jnp.sum((c_start[:, None] >= pends[None, :]).astype(I32), axis=1), e - 1)
    r = jnp.arange(c, dtype=I32)[None, :]
    off = (c_start - pstarts[e_chunk])[:, None] + r
    valid = (off < counts[e_chunk][:, None]) & (c_start < pends[-1])[:, None]
    a_p = order[jnp.where(valid, starts[e_chunk][:, None] + off, 0)]
    tok_pad = (jnp.where(valid, a_p % t, 0) * TOK_SLAB).astype(I32)
    dst_pad = (jnp.where(valid, a_p, m + r) * TOK_SLAB).astype(I32)
    n_active = (pends[-1] // c).astype(I32).reshape(1)
    return e_chunk.astype(I32), n_active, tok_pad[:, None, :], dst_pad[:, None, :], m


def _alibi_slopes(n_heads):
    return 2.0 ** (-8.0 * jnp.arange(1, n_heads + 1, dtype=F32) / n_heads)


def _bf16_parts(v):
    p1 = v.astype(BF16).astype(F32)
    p2 = (v - p1).astype(BF16).astype(F32)
    p3 = (v - p1 - p2).astype(BF16).astype(F32)
    return jnp.stack([p1, p2, p3], axis=1)


def _layer(x, c_pad, ln1_gain, ln2_gain, w_ada, b_ada, w_in, moba_q_gain, moba_k_gain, dsa_q_gain,
           dsa_k_gain, idx_k_gain, w_proj_moba, w_proj_dsa, w_out, w_router, router_bias,
           w1_experts, w3_experts, w2_experts, w1_shared, w3_shared, w2_shared):
    b, s, d = x.shape
    t = b * s
    mw = MOBA_HEADS * HEAD_DIM
    qw = DSA_HEADS * HEAD_DIM
    kvw = DSA_KV_HEADS * HEAD_DIM
    iw = IDX_HEADS * IDX_DIM

    mod = _ada(c_pad, w_ada, b_ada.reshape(1, -1))[:b]
    sh1, sc1, g1, sh2, sc2, g2 = jnp.split(mod, 6, axis=-1)

    (h,) = _normmod(x, ln1_gain, sc1, sh1, with_packed=False)

    o_qa, o_ka, o_va = 0, mw, 2 * mw
    o_qb = 3 * mw
    o_kb = o_qb + qw
    o_vb = o_kb + kvw
    o_qi = o_vb + kvw
    o_ki = o_qi + iw
    o_wi = o_ki + IDX_DIM
    o_ga = o_wi + IDX_HEADS
    o_gb = o_ga + d
    cols = lambda a, n: w_in[:, a:a + n]

    scale = HEAD_DIM ** -0.5 * LOG2E
    w_qk = jnp.concatenate([cols(o_qa, mw), cols(o_ka, mw), cols(o_qb, qw), cols(o_kb, kvw)], axis=1).astype(BF16)
    g_qk = jnp.concatenate([jnp.tile(moba_q_gain * scale, MOBA_HEADS), jnp.tile(moba_k_gain, MOBA_HEADS),
                            jnp.tile(dsa_q_gain * scale, DSA_HEADS), jnp.tile(dsa_k_gain, DSA_KV_HEADS)])
    qk = _proj(h, w_qk, g_qk.reshape(1, -1), "headnorm", BF16)
    w_vq = jnp.concatenate([cols(o_qi, iw), cols(o_va, mw), cols(o_vb, kvw)], axis=1).astype(BF16)
    vq = _proj(h, w_vq, jnp.zeros((1, w_vq.shape[1]), F32), "plain", BF16)
    w_gate = cols(o_ga, 2 * d).astype(BF16)
    gates = _proj(h, w_gate, jnp.zeros((1, 2 * d), F32), "sigmoid", BF16)
    w_ix = jnp.pad(cols(o_ki, IDX_DIM + IDX_HEADS), ((0, 0), (0, LANES - IDX_DIM - IDX_HEADS))).astype(BF16)
    aux_ix = jnp.concatenate([idx_k_gain, jnp.full((IDX_HEADS,), IDX_HEADS ** -0.5 * IDX_DIM ** -0.5, F32),
                              jnp.zeros((LANES - IDX_DIM - IDX_HEADS,), F32)])
    kiw = _proj(h, w_ix, aux_ix.reshape(1, -1), "indexer", F32, tn_pref=LANES)

    hd = HEAD_DIM
    o_a = _moba(_bf16_parts(_alibi_slopes(MOBA_HEADS) * LOG2E), qk, vq, q_col0=0, k_col0=mw // hd, v_col0=iw // hd)

    ki = kiw[:, :, :IDX_DIM].astype(BF16)
    wi = kiw[:, :, IDX_DIM:IDX_DIM + IDX_HEADS]
    bias = _dsa_select(vq, 0, ki, wi)
    o_b = _dsa_attn(_bf16_parts(_alibi_slopes(DSA_HEADS) * LOG2E), qk, vq, bias, q_col0=2 * mw // hd,
                    k_col0=(2 * mw + qw) // hd, v_col0=(iw + mw) // hd)

    merged = _merge(o_a, w_proj_moba.astype(BF16), o_b, w_proj_dsa.astype(BF16), gates)
    x1 = _resid_mm(merged, w_out.astype(BF16), x, g1)

    h2, h2_packed = _normmod(x1, ln2_gain, sc2, sh2, with_packed=True)
    wr_t = w_router.T.astype(BF16)
    idx_t, wgt_t = _router(h2.reshape(t, d), wr_t, router_bias.reshape(-1, 1))
    e_chunk, n_active, tok_pad, dst_pad, n_assign = _dispatch(idx_t)
    yk = _moe(e_chunk, n_active, tok_pad, dst_pad, h2_packed.reshape(t * TOK_SLAB, LANES),
              w1_experts.astype(BF16), w3_experts.astype(BF16), w2_experts.astype(BF16), n_assign)
    hs = _swiglu_up(h2, w1_shared.astype(BF16), w3_shared.astype(BF16))
    return _final(hs, w2_shared.astype(BF16), yk, wgt_t.T, x1, g2)


def kernel(x, c, ln1_gain, ln2_gain, w_ada, b_ada, w_in, moba_q_gain, moba_k_gain, dsa_q_gain, dsa_k_gain, idx_k_gain, w_proj_moba, w_proj_dsa, w_out, w_router, router_bias, w1_experts, w3_experts, w2_experts, w1_shared, w3_shared, w2_shared):
    depth = w_ada.shape[0]
    b = x.shape[0]
    c_pad = jnp.pad(c, ((0, (-b) % 8), (0, 0)))
    params = (ln1_gain, ln2_gain, w_ada, b_ada, w_in, moba_q_gain, moba_k_gain, dsa_q_gain, dsa_k_gain,
              idx_k_gain, w_proj_moba, w_proj_dsa, w_out, w_router, router_bias, w1_experts, w3_experts,
              w2_experts, w1_shared, w3_shared, w2_shared)
    for l in range(depth):
        x = _layer(x, c_pad, *[p[l] for p in params])
    return x
```

```python
import functools

import jax
import jax.numpy as jnp
from jax import lax
from jax.experimental import pallas as pl
from jax.experimental.pallas import tpu as pltpu

F32 = jnp.float32
BF16 = jnp.bfloat16
I32 = jnp.int32
U32 = jnp.uint32

HEAD_DIM = 128
MOBA_HEADS = 16
MOBA_BLOCK = 256
MOBA_TOPK = 3
DSA_HEADS = 16
DSA_KV_HEADS = 4
DSA_TOPK_MAX = 256
IDX_HEADS = 32
IDX_DIM = 64
N_EXPERTS = 64
EXPERT_TOPK = 8
N_GROUPS = 8
TOPK_GROUPS = 4
ROUTED_SCALE = 2.5
NORM_EPS = 1e-6

LANES = 128
V7X_VMEM_LIMIT_BYTES = 56 * 1024 * 1024

MOE_CHUNK = 256
YK_PACK_TILE = 512
NEG = -1e30
INT_MIN = -2 ** 31
LOG2E = 1.4426950408889634

_NT = (((1,), (1,)), ((), ()))


def _cparams(sem):
    return pltpu.CompilerParams(dimension_semantics=sem, vmem_limit_bytes=V7X_VMEM_LIMIT_BYTES)


def _tile(n, pref):
    if n <= pref:
        return n
    t = pref - pref % LANES
    while t > LANES and n % t:
        t -= LANES
    assert n % t == 0, (n, pref)
    return t


def _ada_kernel(c_ref, w_ref, b_ref, o_ref):
    c = c_ref[...]
    s = c * jax.nn.sigmoid(c)
    acc = jnp.dot(s.astype(BF16), w_ref[...].astype(BF16), preferred_element_type=F32)
    o_ref[...] = acc + b_ref[...]


def _ada(c_pad, w_ada, b_ada):
    rows, d = c_pad.shape
    n = w_ada.shape[1]
    tn = _tile(n, 512)
    return pl.pallas_call(
        _ada_kernel,
        grid=(n // tn,),
        in_specs=[pl.BlockSpec((rows, d), lambda j: (0, 0)),
                  pl.BlockSpec((d, tn), lambda j: (0, j)),
                  pl.BlockSpec((1, tn), lambda j: (0, j))],
        out_specs=pl.BlockSpec((rows, tn), lambda j: (0, j)),
        out_shape=jax.ShapeDtypeStruct((rows, n), F32),
        compiler_params=_cparams(("arbitrary",)),
        name="ada_mod",
    )(c_pad, w_ada, b_ada)


def _pack_bf16_pairs(v):
    n = v.shape[1] // 2
    bits = lax.bitcast_convert_type(v.astype(F32), U32)
    return (bits[:, :n] >> 16) | bits[:, n:]


def _unpack_bf16_pairs(w):
    lo = lax.bitcast_convert_type(w << 16, F32)
    hi = lax.bitcast_convert_type(w & jnp.uint32(0xFFFF0000), F32)
    return lo, hi


def _normmod_kernel(x_ref, gain_ref, sc_ref, sh_ref, *o_refs):
    x = x_ref[0]
    y = x * lax.rsqrt(jnp.mean(x * x, axis=-1, keepdims=True) + NORM_EPS) * gain_ref[...]
    h = (y * (1.0 + sc_ref[0]) + sh_ref[0]).astype(BF16)
    o_refs[0][0] = h
    if len(o_refs) > 1:
        words = _pack_bf16_pairs(h)
        ts = words.shape[0]
        slab_rows = words.shape[1] // LANES
        for a in range(slab_rows):
            o_refs[1][pl.ds(a, ts, stride=slab_rows), :] = words[:, a * LANES:(a + 1) * LANES]


def _normmod(x, gain, sc, sh, with_packed):
    b, s, d = x.shape
    ts = _tile(s, 256)
    blk = pl.BlockSpec((1, ts, d), lambda bi, i: (bi, i, 0))
    vec = pl.BlockSpec((1, 1, d), lambda bi, i: (bi, 0, 0))
    out_shape = [jax.ShapeDtypeStruct((b, s, d), BF16)]
    out_specs = [blk]
    if with_packed:
        slab_rows = d // 2 // LANES
        out_shape.append(jax.ShapeDtypeStruct((b * s * slab_rows, LANES), U32))
        out_specs.append(pl.BlockSpec((ts * slab_rows, LANES), lambda bi, i: (bi * (s // ts) + i, 0)))
    return pl.pallas_call(
        _normmod_kernel,
        grid=(b, s // ts),
        in_specs=[blk, pl.BlockSpec((1, d), lambda bi, i: (0, 0)), vec, vec],
        out_specs=out_specs,
        out_shape=out_shape,
        compiler_params=_cparams(("arbitrary", "arbitrary")),
        name="norm_modulate",
    )(x, gain.reshape(1, d), sc.reshape(b, 1, d), sh.reshape(b, 1, d))


def _proj_kernel(a_ref, w_ref, aux_ref, o_ref, *, mode):
    acc = jnp.dot(a_ref[0], w_ref[...], preferred_element_type=F32)
    tn = acc.shape[1]
    if mode == "plain":
        o_ref[0] = acc.astype(o_ref.dtype)
    elif mode == "sigmoid":
        o_ref[0] = jax.nn.sigmoid(acc).astype(o_ref.dtype)
    elif mode == "headnorm":
        for g in range(tn // HEAD_DIM):
            sl = slice(g * HEAD_DIM, (g + 1) * HEAD_DIM)
            blk = acc[:, sl]
            ms = jnp.mean(blk * blk, axis=-1, keepdims=True)
            o_ref[0, :, sl] = (blk * lax.rsqrt(ms + NORM_EPS) * aux_ref[:, sl]).astype(o_ref.dtype)
    elif mode == "indexer":
        lane = lax.broadcasted_iota(I32, acc.shape, 1)
        is_key = lane < IDX_DIM
        ms = jnp.sum(jnp.where(is_key, acc * acc, 0.0), axis=-1, keepdims=True) / IDX_DIM
        o_ref[0] = acc * jnp.where(is_key, lax.rsqrt(ms + NORM_EPS), 1.0) * aux_ref[...]
    else:
        raise ValueError(mode)


def _proj(a, w, aux, mode, out_dtype, tm_pref=1024, tn_pref=512):
    b, s, k = a.shape
    n = w.shape[1]
    tm, tn = _tile(s, tm_pref), _tile(n, tn_pref)
    return pl.pallas_call(
        functools.partial(_proj_kernel, mode=mode),
        grid=(b, s // tm, n // tn),
        in_specs=[pl.BlockSpec((1, tm, k), lambda bi, i, j: (bi, i, 0)),
                  pl.BlockSpec((k, tn), lambda bi, i, j: (0, j)),
                  pl.BlockSpec((1, tn), lambda bi, i, j: (0, j))],
        out_specs=pl.BlockSpec((1, tm, tn), lambda bi, i, j: (bi, i, j)),
        out_shape=jax.ShapeDtypeStruct((b, s, n), out_dtype),
        compiler_params=_cparams(("arbitrary", "arbitrary", "arbitrary")),
        name="proj_" + mode,
    )(a, w, aux)


AUG_ALIBI = 0
AUG_SEL = 8
MOBA_QBLOCKS = 2


def _moba_kernel(sp_ref, q_ref, k_ref, v_ref, o_ref, kaug_sc, vt_sc, kmean_sc, *, nb, gt, qb):
    h = pl.program_id(1)
    ip = pl.program_id(2)
    blk = MOBA_BLOCK
    tq = qb * blk
    nbp = kmean_sc.shape[0]
    gblocks = gt // blk

    @pl.when(ip == 0)
    def _():
        kmean_sc[...] = jnp.zeros_like(kmean_sc)
        lane = lax.broadcasted_iota(I32, (blk, LANES), 1)
        row = lax.broadcasted_iota(I32, (blk, LANES), 0)
        for n in range(nb):
            rows = slice(n * blk, (n + 1) * blk)
            kb = k_ref[0, rows, :]
            kmean_sc[n:n + 1, :] = jnp.mean(kb.astype(F32), axis=0, keepdims=True)
            aug = jnp.where(lane == AUG_SEL + n, 1.0, _alibi_key_aug(n * blk + row, lane))
            kaug_sc[rows, :HEAD_DIM] = kb
            kaug_sc[rows, HEAD_DIM:] = aug.astype(BF16)
            vt_sc[:, rows] = v_ref[0, rows, :].astype(F32).T.astype(BF16)

    q = q_ref[0]
    gate_t = lax.dot_general(kmean_sc[...], q.astype(F32), _NT, preferred_element_type=F32,
                             precision=lax.Precision.HIGHEST)
    r_iota = lax.broadcasted_iota(I32, gate_t.shape, 0)
    assert blk & (blk - 1) == 0
    own = ip * qb + (lax.broadcasted_iota(I32, gate_t.shape, 1) >> (blk.bit_length() - 1))
    rank = jnp.zeros(gate_t.shape, I32)
    for m in range(nb):
        gm = gate_t[m:m + 1, :]
        beats = ((gm > gate_t) | ((gm == gate_t) & (m < r_iota))) & (m < own)
        rank = rank + jnp.where(beats, 1, 0)
    keep_t = ((r_iota < own) & (rank < MOBA_TOPK)) | (r_iota == own)
    selb_t = jnp.where(keep_t, 0.0, NEG)

    augq_t = jnp.concatenate([_alibi_query_aug(sp_ref, h, (8, tq), 0), selb_t,
                              jnp.zeros((LANES - 8 - nbp, tq), F32)], axis=0)
    qa = jnp.concatenate([q, augq_t.T.astype(BF16)], axis=1)

    t_pos = ip * tq + lax.broadcasted_iota(I32, (gt, tq), 1)
    s_off = lax.broadcasted_iota(I32, (gt, tq), 0)

    def group(gi, carry, causal):
        m_prev, l_prev, acc_prev = carry
        g0 = pl.multiple_of(gi * gt, gt)
        st = lax.dot_general(kaug_sc[pl.ds(g0, gt), :], qa, _NT, preferred_element_type=F32)
        if causal:
            st = jnp.where(g0 + s_off <= t_pos, st, NEG)
        m_new = jnp.maximum(m_prev, jnp.max(st, axis=0, keepdims=True))
        alpha = jnp.exp2(m_prev - m_new)
        p = jnp.exp2(st - m_new)
        l_new = alpha * l_prev + jnp.sum(p, axis=0, keepdims=True)
        pv = jnp.dot(vt_sc[:, pl.ds(g0, gt)], p.astype(BF16), preferred_element_type=F32)
        return m_new, l_new, alpha * acc_prev + pv

    init = (jnp.full((1, tq), NEG, F32), jnp.zeros((1, tq), F32), jnp.zeros((HEAD_DIM, tq), F32))
    n_full = (ip * qb) // gblocks
    carry = lax.fori_loop(0, n_full, lambda gi, c: group(gi, c, False), init)
    _, l_fin, acc_fin = group(n_full, carry, True)
    o_ref[0] = (acc_fin / l_fin).T.astype(o_ref.dtype)


def _moba(slope_parts, qk, vq, q_col0, k_col0, v_col0):
    b, s, _ = qk.shape
    assert s % MOBA_BLOCK == 0 and s <= 4096
    nb = s // MOBA_BLOCK
    nbp = -(-nb // 8) * 8
    assert AUG_SEL + nbp <= LANES
    gt = min(s, 4 * MOBA_BLOCK)
    qb = MOBA_QBLOCKS
    assert s % gt == 0 and nb % qb == 0 and (gt // MOBA_BLOCK) % qb == 0
    tq = qb * MOBA_BLOCK
    return pl.pallas_call(
        functools.partial(_moba_kernel, nb=nb, gt=gt, qb=qb),
        grid=(b, MOBA_HEADS, nb // qb),
        in_specs=[pl.BlockSpec(memory_space=pltpu.SMEM),
                  pl.BlockSpec((1, tq, HEAD_DIM), lambda bi, h, i: (bi, i, q_col0 + h)),
                  pl.BlockSpec((1, s, HEAD_DIM), lambda bi, h, i: (bi, 0, k_col0 + h)),
                  pl.BlockSpec((1, s, HEAD_DIM), lambda bi, h, i: (bi, 0, v_col0 + h))],
        out_specs=pl.BlockSpec((1, tq, HEAD_DIM), lambda bi, h, i: (bi, i, h)),
        out_shape=jax.ShapeDtypeStruct((b, s, MOBA_HEADS * HEAD_DIM), BF16),
        scratch_shapes=[pltpu.VMEM((s, 2 * HEAD_DIM), BF16),
                        pltpu.VMEM((HEAD_DIM, s), BF16),
                        pltpu.VMEM((nbp, HEAD_DIM), F32)],
        compiler_params=_cparams(("arbitrary", "arbitrary", "arbitrary")),
        name="moba_attention",
    )(slope_parts, qk, qk, vq)


def _dsa_select_kernel(qi_ref, ki_ref, wi_ref, bias_ref, key_sc, qs_sc, *, tq, tk, nsel, hgroup):
    j = pl.program_id(1)
    t0 = j * tq
    nkc = (t0 + tq + tk - 1) // tk
    n_hg = IDX_HEADS // hgroup

    bias_ref[...] = jnp.full(bias_ref.shape, NEG, bias_ref.dtype)
    for hd in range(IDX_HEADS):
        qs_sc[hd * tq:(hd + 1) * tq, :] = qi_ref[0, :, hd * IDX_DIM:(hd + 1) * IDX_DIM]
    w = wi_ref[0]
    row_t = t0 + lax.broadcasted_iota(I32, (tq, tk), 0)
    col_i = lax.broadcasted_iota(I32, (tq, tk), 1)

    def score_chunk(c, _):
        c0 = pl.multiple_of(c * tk, tk)
        kc = ki_ref[0, pl.ds(c0, tk), :]
        acc = jnp.zeros((tq, tk), F32)
        for g in range(n_hg):
            qg = qs_sc[g * hgroup * tq:(g + 1) * hgroup * tq, :]
            d = lax.dot_general(qg, kc, _NT, preferred_element_type=F32)
            for hh in range(hgroup):
                hd = g * hgroup + hh
                acc = acc + w[:, hd:hd + 1] * jnp.maximum(d[hh * tq:(hh + 1) * tq], 0.0)
        bits = lax.bitcast_convert_type(acc, I32)
        key = jnp.where(bits >= 0, bits, bits ^ 0x7FFFFFFF)
        key = jnp.where(c0 + col_i <= row_t, key, INT_MIN)
        key_sc[:, pl.ds(c0, tk)] = key
        return 0

    lax.fori_loop(0, nkc, score_chunk, 0)

    def bit_body(bi, prefix):
        cand = prefix + jnp.left_shift(jnp.int32(1), 31 - bi)

        def cnt_chunk(c, cnt):
            blk = key_sc[:, pl.ds(pl.multiple_of(c * tk, tk), tk)]
            ge = jnp.where(blk >= cand, 1.0, 0.0)
            for g in range(tk // LANES):
                cnt = cnt + ge[:, g * LANES:(g + 1) * LANES]
            return cnt

        cnt = lax.fori_loop(0, nkc, cnt_chunk, jnp.zeros((tq, LANES), F32))
        total = jnp.sum(cnt, axis=-1, keepdims=True)
        return jnp.where(total >= nsel, cand, prefix)

    thr = lax.fori_loop(0, 32, bit_body, jnp.full((tq, 1), INT_MIN, I32))

    def out_chunk(c, _):
        c0 = pl.multiple_of(c * tk, tk)
        blk = key_sc[:, pl.ds(c0, tk)]
        keep = (blk >= thr) & (c0 + col_i <= row_t)
        bias_ref[0, :, pl.ds(c0, tk)] = jnp.where(keep, 0.0, NEG).astype(bias_ref.dtype)
        return 0

    lax.fori_loop(0, nkc, out_chunk, 0)


def _dsa_select(vq, qi_blk, ki, wi):
    b, s, _ = vq.shape
    iw = IDX_HEADS * IDX_DIM
    tq = _tile(s, 128)
    tk = _tile(s, 512)
    nsel = min(DSA_TOPK_MAX, s // 4)
    return pl.pallas_call(
        functools.partial(_dsa_select_kernel, tq=tq, tk=tk, nsel=nsel, hgroup=8),
        grid=(b, s // tq),
        in_specs=[pl.BlockSpec((1, tq, iw), lambda bi, j: (bi, j, qi_blk)),
                  pl.BlockSpec((1, s, IDX_DIM), lambda bi, j: (bi, 0, 0)),
                  pl.BlockSpec((1, tq, IDX_HEADS), lambda bi, j: (bi, j, 0))],
        out_specs=pl.BlockSpec((1, tq, s), lambda bi, j: (bi, j, 0)),
        out_shape=jax.ShapeDtypeStruct((b, s, s), BF16),
        scratch_shapes=[pltpu.VMEM((tq, s), I32), pltpu.VMEM((IDX_HEADS * tq, IDX_DIM), BF16)],
        compiler_params=_cparams(("arbitrary", "arbitrary")),
        name="dsa_select",
    )(vq, ki, wi)


def _alibi_key_aug(pos, lane):
    alibi = jnp.where((lane & 1) == 0, ((pos >> 4) << 4).astype(F32), (pos & 15).astype(F32))
    return jnp.where(lane < AUG_ALIBI + 6, alibi, 0.0)


def _alibi_query_aug(sp_ref, h, shape, axis):
    ix = lax.broadcasted_iota(I32, shape, axis)
    return jnp.where(ix < 2, sp_ref[h, 0], jnp.where(ix < 4, sp_ref[h, 1], jnp.where(ix < 6, sp_ref[h, 2], 0.0)))


def _dsa_attn_kernel(sp_ref, q_ref, k_ref, v_ref, bias_ref, o_ref, kaug_sc, *, tq, tk, grp):
    g = pl.program_id(1)
    j = pl.program_id(2)
    t0 = j * tq
    nk = (t0 + tq + tk - 1) // tk
    rows = grp * tq
    dh = HEAD_DIM
    s_len = kaug_sc.shape[0]

    @pl.when(j == 0)
    def _():
        lane = lax.broadcasted_iota(I32, (tk, LANES), 1)
        row = lax.broadcasted_iota(I32, (tk, LANES), 0)
        for n in range(s_len // tk):
            rs = slice(n * tk, (n + 1) * tk)
            kaug_sc[rs, :dh] = k_ref[0, rs, :]
            kaug_sc[rs, dh:] = _alibi_key_aug(n * tk + row, lane).astype(BF16)

    q = jnp.concatenate(
        [jnp.concatenate([q_ref[0, :, r * dh:(r + 1) * dh],
                          _alibi_query_aug(sp_ref, g * grp + r, (tq, LANES), 1).astype(BF16)], axis=1)
         for r in range(grp)], axis=0)

    def body(c, carry):
        m_prev, l_prev, acc_prev = carry
        c0 = pl.multiple_of(c * tk, tk)
        s = lax.dot_general(q, kaug_sc[pl.ds(c0, tk), :], _NT, preferred_element_type=F32)
        mb = bias_ref[0, :, pl.ds(c0, tk)].astype(F32)
        lg = jnp.concatenate([s[r * tq:(r + 1) * tq] + mb for r in range(grp)], axis=0)
        m_new = jnp.maximum(m_prev, jnp.max(lg, axis=-1, keepdims=True))
        alpha = jnp.exp2(m_prev - m_new)
        p = jnp.exp2(lg - m_new)
        l_new = alpha * l_prev + jnp.sum(p, axis=-1, keepdims=True)
        pv = jnp.dot(p.astype(BF16), v_ref[0, pl.ds(c0, tk), :], preferred_element_type=F32)
        return m_new, l_new, alpha * acc_prev + pv

    init = (jnp.full((rows, 1), NEG, F32), jnp.zeros((rows, 1), F32), jnp.zeros((rows, dh), F32))
    _, l_fin, acc_fin = lax.fori_loop(0, nk, body, init)
    out = acc_fin / l_fin
    for r in range(grp):
        o_ref[0, :, r * dh:(r + 1) * dh] = out[r * tq:(r + 1) * tq].astype(o_ref.dtype)


def _dsa_attn(slope_parts, qk, vq, bias, q_col0, k_col0, v_col0):
    b, s, _ = qk.shape
    assert s <= 4096
    grp = DSA_HEADS // DSA_KV_HEADS
    gw = grp * HEAD_DIM
    assert (q_col0 * HEAD_DIM) % gw == 0
    q_blk0 = q_col0 * HEAD_DIM // gw
    tq = _tile(s, 128)
    tk = _tile(s, 512)
    return pl.pallas_call(
        functools.partial(_dsa_attn_kernel, tq=tq, tk=tk, grp=grp),
        grid=(b, DSA_KV_HEADS, s // tq),
        in_specs=[pl.BlockSpec(memory_space=pltpu.SMEM),
                  pl.BlockSpec((1, tq, gw), lambda bi, g, j: (bi, j, q_blk0 + g)),
                  pl.BlockSpec((1, s, HEAD_DIM), lambda bi, g, j: (bi, 0, k_col0 + g)),
                  pl.BlockSpec((1, s, HEAD_DIM), lambda bi, g, j: (bi, 0, v_col0 + g)),
                  pl.BlockSpec((1, tq, s), lambda bi, g, j: (bi, j, 0))],
        out_specs=pl.BlockSpec((1, tq, gw), lambda bi, g, j: (bi, j, g)),
        out_shape=jax.ShapeDtypeStruct((b, s, DSA_HEADS * HEAD_DIM), BF16),
        scratch_shapes=[pltpu.VMEM((s, 2 * HEAD_DIM), BF16)],
        compiler_params=_cparams(("arbitrary", "arbitrary", "arbitrary")),
        name="dsa_attention",
    )(slope_parts, qk, qk, vq, bias)


def _merge_kernel(oa_ref, wa_ref, ob_ref, wb_ref, ga_ref, gb_ref, o_ref):
    ya = jnp.dot(oa_ref[0], wa_ref[...], preferred_element_type=F32)
    yb = jnp.dot(ob_ref[0], wb_ref[...], preferred_element_type=F32)
    o_ref[0] = (ga_ref[0].astype(F32) * ya + gb_ref[0].astype(F32) * yb).astype(o_ref.dtype)


def _merge(oa, wa, ob, wb, gates):
    b, s, ka = oa.shape
    kb = ob.shape[2]
    n = wa.shape[1]
    tm, tn = _tile(s, 1024), _tile(n, 512)
    nj = n // tn
    return pl.pallas_call(
        _merge_kernel,
        grid=(b, s // tm, nj),
        in_specs=[pl.BlockSpec((1, tm, ka), lambda bi, i, j: (bi, i, 0)),
                  pl.BlockSpec((ka, tn), lambda bi, i, j: (0, j)),
                  pl.BlockSpec((1, tm, kb), lambda bi, i, j: (bi, i, 0)),
                  pl.BlockSpec((kb, tn), lambda bi, i, j: (0, j)),
                  pl.BlockSpec((1, tm, tn), lambda bi, i, j: (bi, i, j)),
                  pl.BlockSpec((1, tm, tn), lambda bi, i, j: (bi, i, nj + j))],
        out_specs=pl.BlockSpec((1, tm, tn), lambda bi, i, j: (bi, i, j)),
        out_shape=jax.ShapeDtypeStruct((b, s, n), BF16),
        compiler_params=_cparams(("arbitrary", "arbitrary", "arbitrary")),
        name="gated_merge",
    )(oa, wa, ob, wb, gates, gates)


def _resid_kernel(a_ref, w_ref, x_ref, g_ref, o_ref):
    y = jnp.dot(a_ref[0], w_ref[...], preferred_element_type=F32)
    o_ref[0] = x_ref[0] + g_ref[0] * y


def _resid_mm(a, w, x, g):
    b, s, k = a.shape
    n = w.shape[1]
    tm, tn = _tile(s, 1024), _tile(n, 512)
    return pl.pallas_call(
        _resid_kernel,
        grid=(b, s // tm, n // tn),
        in_specs=[pl.BlockSpec((1, tm, k), lambda bi, i, j: (bi, i, 0)),
                  pl.BlockSpec((k, tn), lambda bi, i, j: (0, j)),
                  pl.BlockSpec((1, tm, tn), lambda bi, i, j: (bi, i, j)),
                  pl.BlockSpec((1, 1, tn), lambda bi, i, j: (bi, 0, j))],
        out_specs=pl.BlockSpec((1, tm, tn), lambda bi, i, j: (bi, i, j)),
        out_shape=jax.ShapeDtypeStruct((b, s, n), F32),
        compiler_params=_cparams(("arbitrary", "arbitrary", "arbitrary")),
        name="out_proj_residual",
    )(a, w, x, g.reshape(b, 1, n))


def _swiglu_kernel(a_ref, w1_ref, w3_ref, o_ref):
    a = a_ref[0]
    u = jnp.dot(a, w1_ref[...], preferred_element_type=F32)
    v = jnp.dot(a, w3_ref[...], preferred_element_type=F32)
    o_ref[0] = (u * jax.nn.sigmoid(u) * v).astype(o_ref.dtype)


def _swiglu_up(a, w1, w3):
    b, s, k = a.shape
    n = w1.shape[1]
    tm, tn = _tile(s, 1024), _tile(n, 512)
    return pl.pallas_call(
        _swiglu_kernel,
        grid=(b, s // tm, n // tn),
        in_specs=[pl.BlockSpec((1, tm, k), lambda bi, i, j: (bi, i, 0)),
                  pl.BlockSpec((k, tn), lambda bi, i, j: (0, j)),
                  pl.BlockSpec((k, tn), lambda bi, i, j: (0, j))],
        out_specs=pl.BlockSpec((1, tm, tn), lambda bi, i, j: (bi, i, j)),
        out_shape=jax.ShapeDtypeStruct((b, s, n), BF16),
        compiler_params=_cparams(("arbitrary", "arbitrary", "arbitrary")),
        name="shared_swiglu_up",
    )(a, w1, w3)


def _final_kernel(a_ref, w_ref, *rest):
    yk_refs, (rw_ref, x_ref, g_ref, o_ref) = rest[:EXPERT_TOPK], rest[EXPERT_TOPK:]
    y = jnp.dot(a_ref[0], w_ref[...], preferred_element_type=F32)
    lo = hi = None
    for kk, r in enumerate(yk_refs):
        lo_r, hi_r = _unpack_bf16_pairs(r[...])
        wk = rw_ref[:, kk:kk + 1]
        lo = lo_r * wk if lo is None else lo + lo_r * wk
        hi = hi_r * wk if hi is None else hi + hi_r * wk
    o_ref[0] = x_ref[0] + g_ref[0] * (y + jnp.concatenate([lo, hi], axis=1))


def _final(hs, w2, yk, route_w, x, g):
    b, s, k = hs.shape
    n = w2.shape[1]
    tm, tn = _tile(s, 512), YK_PACK_TILE
    assert n % tn == 0
    nti = s // tm
    t_blocks = b * nti
    yk_specs = [pl.BlockSpec((tm, tn // 2), lambda bi, i, j, kk=kk: (kk * t_blocks + bi * nti + i, j))
                for kk in range(EXPERT_TOPK)]
    return pl.pallas_call(
        _final_kernel,
        grid=(b, nti, n // tn),
        in_specs=[pl.BlockSpec((1, tm, k), lambda bi, i, j: (bi, i, 0)),
                  pl.BlockSpec((k, tn), lambda bi, i, j: (0, j))] + yk_specs + [
                  pl.BlockSpec((tm, EXPERT_TOPK), lambda bi, i, j: (bi * nti + i, 0)),
                  pl.BlockSpec((1, tm, tn), lambda bi, i, j: (bi, i, j)),
                  pl.BlockSpec((1, 1, tn), lambda bi, i, j: (bi, 0, j))],
        out_specs=pl.BlockSpec((1, tm, tn), lambda bi, i, j: (bi, i, j)),
        out_shape=jax.ShapeDtypeStruct((b, s, n), F32),
        compiler_params=_cparams(("arbitrary", "arbitrary", "arbitrary")),
        name="moe_combine_residual",
    )(hs, w2, *([yk] * EXPERT_TOPK), route_w, x, g.reshape(b, 1, n))


def _router_kernel(h_ref, wr_ref, b_ref, idx_ref, wgt_ref):
    tm = h_ref.shape[0]
    per = N_EXPERTS // N_GROUPS
    logits = lax.dot_general(wr_ref[...], h_ref[...], _NT, preferred_element_type=F32)
    scores = jax.nn.sigmoid(logits)
    choice = scores + b_ref[...]
    sub = lax.broadcasted_iota(I32, (per, tm), 0).astype(F32)
    gs = []
    for g in range(N_GROUPS):
        cg = choice[g * per:(g + 1) * per]
        m1 = jnp.max(cg, axis=0, keepdims=True)
        i1 = jnp.min(jnp.where(cg == m1, sub, float(per)), axis=0, keepdims=True)
        m2 = jnp.max(jnp.where(sub == i1, -jnp.inf, cg), axis=0, keepdims=True)
        gs.append(m1 + m2)
    masked = []
    for g in range(N_GROUPS):
        rank = jnp.zeros((1, tm), F32)
        for g2 in range(N_GROUPS):
            if g2 == g:
                continue
            beats = (gs[g2] > gs[g]) | ((gs[g2] == gs[g]) & (g2 < g))
            rank = rank + jnp.where(beats, 1.0, 0.0)
        masked.append(jnp.where(rank < TOPK_GROUPS, choice[g * per:(g + 1) * per], -jnp.inf))
    cm = jnp.concatenate(masked, axis=0)
    erow = lax.broadcasted_iota(I32, (N_EXPERTS, tm), 0).astype(F32)
    idxs, wts = [], []
    for _ in range(EXPERT_TOPK):
        mx = jnp.max(cm, axis=0, keepdims=True)
        ix = jnp.min(jnp.where(cm == mx, erow, float(N_EXPERTS)), axis=0, keepdims=True)
        hit = erow == ix
        idxs.append(ix)
        wts.append(jnp.sum(jnp.where(hit, scores, 0.0), axis=0, keepdims=True))
        cm = jnp.where(hit, -jnp.inf, cm)
    wt = jnp.concatenate(wts, axis=0)
    idx_ref[...] = jnp.concatenate(idxs, axis=0).astype(I32)
    wgt_ref[...] = wt / jnp.sum(wt, axis=0, keepdims=True) * ROUTED_SCALE


def _router(h2_flat, wr_t, bias_col):
    t, d = h2_flat.shape
    tm = _tile(t, 512)
    return pl.pallas_call(
        _router_kernel,
        grid=(t // tm,),
        in_specs=[pl.BlockSpec((tm, d), lambda i: (i, 0)),
                  pl.BlockSpec((N_EXPERTS, d), lambda i: (0, 0)),
                  pl.BlockSpec((N_EXPERTS, 1), lambda i: (0, 0))],
        out_specs=[pl.BlockSpec((EXPERT_TOPK, tm), lambda i: (0, i)),
                   pl.BlockSpec((EXPERT_TOPK, tm), lambda i: (0, i))],
        out_shape=[jax.ShapeDtypeStruct((EXPERT_TOPK, t), I32),
                   jax.ShapeDtypeStruct((EXPERT_TOPK, t), F32)],
        compiler_params=_cparams(("arbitrary",)),
        name="router_topk",
    )(h2_flat, wr_t, bias_col)


MOE_DMA_UNROLL = 8
DMA_PRIORITIES = 2


def _moe_kernel(ech_ref, nact_ref, tokc_ref, tokn_ref, dst_ref, h_hbm, w1_ref, w3_ref, w2_ref,
                yk_hbm, xs, ys, gsem, ssem, *, dump_row0):
    del ech_ref
    c = pl.program_id(0)
    nc = pl.num_programs(0)
    nact = nact_ref[0]
    rows = MOE_CHUNK
    sl = xs.shape[1] // rows
    slot = c % 2

    def gather_copy(tok_ref, r, s):
        return pltpu.make_async_copy(h_hbm.at[pl.ds(pl.multiple_of(tok_ref[0, 0, r], sl), sl), :],
                                     xs.at[s, pl.ds(pl.multiple_of(r * sl, sl), sl), :], gsem.at[s])

    def scatter_copy(r, dst_row):
        return pltpu.make_async_copy(ys.at[pl.ds(r, 1), :], yk_hbm.at[pl.ds(dst_row, 1), :], ssem.at[0])

    def per_row(fn):
        def body(g, carry):
            for j in range(MOE_DMA_UNROLL):
                fn(g * MOE_DMA_UNROLL + j, j % DMA_PRIORITIES)
            return carry
        lax.fori_loop(0, rows // MOE_DMA_UNROLL, body, 0)

    def wait_scatter():
        per_row(lambda r, p: scatter_copy(r, 0).wait())

    @pl.when(c == 0)
    def _():
        ys[...] = jnp.zeros_like(ys)
        per_row(lambda r, p: scatter_copy(r, dump_row0 + r).start(priority=p))
        per_row(lambda r, p: gather_copy(tokc_ref, r, 0).start(priority=p))

    @pl.when(c < nact)
    def _():
        per_row(lambda r, p: gather_copy(tokn_ref, r, 1 - slot).start(priority=p))
        per_row(lambda r, p: gather_copy(tokc_ref, r, slot).wait())

        xsl = xs.at[slot]
        parts = [_unpack_bf16_pairs(xsl[pl.ds(a, rows, stride=sl), :]) for a in range(sl)]
        x = jnp.concatenate([p[0] for p in parts] + [p[1] for p in parts], axis=1).astype(BF16)
        u = jnp.dot(x, w1_ref[0], preferred_element_type=F32)
        v = jnp.dot(x, w3_ref[0], preferred_element_type=F32)
        hm = (u * jax.nn.sigmoid(u) * v).astype(BF16)
        d = w2_ref.shape[2]
        half = YK_PACK_TILE // 2
        tiles = []
        for j in range(d // YK_PACK_TILE):
            yj = jnp.dot(hm, w2_ref[0, :, j * YK_PACK_TILE:(j + 1) * YK_PACK_TILE],
                         preferred_element_type=F32)
            tiles.append(_pack_bf16_pairs(yj.astype(BF16)))
        wait_scatter()
        for j, tile in enumerate(tiles):
            ys[:, j * half:(j + 1) * half] = tile
        per_row(lambda r, p: scatter_copy(r, dst_ref[0, 0, r]).start(priority=p))

    @pl.when(c == nc - 1)
    def _():
        wait_scatter()
        per_row(lambda r, p: gather_copy(tokc_ref, r, nact % 2).wait())


def _moe(e_chunk, n_active, tok_pad, dst_pad, h2_slabs, w1, w3, w2, n_assign):
    n_chunks = tok_pad.shape[0]
    d, f = w1.shape[1], w1.shape[2]
    dp = d // 2
    sl = dp // LANES
    assert dp % LANES == 0 and d % YK_PACK_TILE == 0 and h2_slabs.shape[1] == LANES
    rows = MOE_CHUNK
    smem_blk = lambda fn: pl.BlockSpec((1, 1, rows), fn, memory_space=pltpu.SMEM)
    grid_spec = pltpu.PrefetchScalarGridSpec(
        num_scalar_prefetch=2,
        grid=(n_chunks,),
        in_specs=[smem_blk(lambda c, ech, na: (c, 0, 0)),
                  smem_blk(lambda c, ech, na: (jnp.minimum(c + 1, n_chunks - 1), 0, 0)),
                  smem_blk(lambda c, ech, na: (c, 0, 0)),
                  pl.BlockSpec(memory_space=pl.ANY),
                  pl.BlockSpec((1, d, f), lambda c, ech, na: (ech[c], 0, 0)),
                  pl.BlockSpec((1, d, f), lambda c, ech, na: (ech[c], 0, 0)),
                  pl.BlockSpec((1, f, d), lambda c, ech, na: (ech[c], 0, 0))],
        out_specs=pl.BlockSpec(memory_space=pl.ANY),
        scratch_shapes=[pltpu.VMEM((2, rows * sl, LANES), U32),
                        pltpu.VMEM((rows, dp), U32),
                        pltpu.SemaphoreType.DMA((2,)),
                        pltpu.SemaphoreType.DMA((1,))],
    )
    return pl.pallas_call(
        functools.partial(_moe_kernel, dump_row0=n_assign),
        grid_spec=grid_spec,
        out_shape=jax.ShapeDtypeStruct((n_assign + rows, dp), U32),
        compiler_params=_cparams(("arbitrary",)),
        name="routed_experts",
    )(e_chunk, n_active, tok_pad, tok_pad, dst_pad, h2_slabs, w1, w3, w2)


def _dispatch(idx_t, slab_rows):
    k, t = idx_t.shape
    e, c = N_EXPERTS, MOE_CHUNK
    m = k * t
    n_chunks = m // c + e
    e_flat = idx_t.reshape(m)
    order = jnp.argsort(e_flat).astype(I32)
    counts = jnp.sum((e_flat[None, :] == jnp.arange(e, dtype=I32)[:, None]).astype(I32), axis=1)
    starts = jnp.cumsum(counts) - counts
    padded = (counts + c - 1) // c * c
    pends = jnp.cumsum(padded)
    pstarts = pends - padded
    c_start = jnp.arange(n_chunks, dtype=I32) * c
    e_chunk = jnp.minimum(jnp.sum((c_start[:, None] >= pends[None, :]).astype(I32), axis=1), e - 1)
    r = jnp.arange(c, dtype=I32)[None, :]
    off = (c_start - pstarts[e_chunk])[:, None] + r
    valid = (off < counts[e_chunk][:, None]) & (c_start < pends[-1])[:, None]
    a_p = order[jnp.where(valid, starts[e_chunk][:, None] + off, 0)]
    tok_pad = (jnp.where(valid, a_p % t, 0) * slab_rows).astype(I32)
    dst_pad = jnp.where(valid, a_p, m + r).astype(I32)
    n_active = (pends[-1] // c).astype(I32).reshape(1)
    return e_chunk.astype(I32), n_active, tok_pad[:, None, :], dst_pad[:, None, :], m


def _alibi_slopes(n_heads):
    return 2.0 ** (-8.0 * jnp.arange(1, n_heads + 1, dtype=F32) / n_heads)


def _bf16_parts(v):
    p1 = v.astype(BF16).astype(F32)
    p2 = (v - p1).astype(BF16).astype(F32)
    p3 = (v - p1 - p2).astype(BF16).astype(F32)
    return jnp.stack([p1, p2, p3], axis=1)


def _layer(x, c_pad, ln1_gain, ln2_gain, w_ada, b_ada, w_in, moba_q_gain, moba_k_gain, dsa_q_gain,
           dsa_k_gain, idx_k_gain, w_proj_moba, w_proj_dsa, w_out, w_router, router_bias,
           w1_experts, w3_experts, w2_experts, w1_shared, w3_shared, w2_shared):
    b, s, d = x.shape
    t = b * s
    mw = MOBA_HEADS * HEAD_DIM
    qw = DSA_HEADS * HEAD_DIM
    kvw = DSA_KV_HEADS * HEAD_DIM
    iw = IDX_HEADS * IDX_DIM

    mod = _ada(c_pad, w_ada, b_ada.reshape(1, -1))[:b]
    sh1, sc1, g1, sh2, sc2, g2 = jnp.split(mod, 6, axis=-1)

    (h,) = _normmod(x, ln1_gain, sc1, sh1, with_packed=False)

    o_qa, o_ka, o_va = 0, mw, 2 * mw
    o_qb = 3 * mw
    o_kb = o_qb + qw
    o_vb = o_kb + kvw
    o_qi = o_vb + kvw
    o_ki = o_qi + iw
    o_wi = o_ki + IDX_DIM
    o_ga = o_wi + IDX_HEADS
    o_gb = o_ga + d
    cols = lambda a, n: w_in[:, a:a + n]

    scale = HEAD_DIM ** -0.5 * LOG2E
    w_qk = jnp.concatenate([cols(o_qa, mw), cols(o_ka, mw), cols(o_qb, qw), cols(o_kb, kvw)], axis=1).astype(BF16)
    g_qk = jnp.concatenate([jnp.tile(moba_q_gain * scale, MOBA_HEADS), jnp.tile(moba_k_gain, MOBA_HEADS),
                            jnp.tile(dsa_q_gain * scale, DSA_HEADS), jnp.tile(dsa_k_gain, DSA_KV_HEADS)])
    qk = _proj(h, w_qk, g_qk.reshape(1, -1), "headnorm", BF16)
    w_vq = jnp.concatenate([cols(o_qi, iw), cols(o_va, mw), cols(o_vb, kvw)], axis=1).astype(BF16)
    vq = _proj(h, w_vq, jnp.zeros((1, w_vq.shape[1]), F32), "plain", BF16)
    w_gate = cols(o_ga, 2 * d).astype(BF16)
    gates = _proj(h, w_gate, jnp.zeros((1, 2 * d), F32), "sigmoid", BF16)
    w_ix = jnp.pad(cols(o_ki, IDX_DIM + IDX_HEADS), ((0, 0), (0, LANES - IDX_DIM - IDX_HEADS))).astype(BF16)
    aux_ix = jnp.concatenate([idx_k_gain, jnp.full((IDX_HEADS,), IDX_HEADS ** -0.5 * IDX_DIM ** -0.5, F32),
                              jnp.zeros((LANES - IDX_DIM - IDX_HEADS,), F32)])
    kiw = _proj(h, w_ix, aux_ix.reshape(1, -1), "indexer", F32, tn_pref=LANES)

    hd = HEAD_DIM
    o_a = _moba(_bf16_parts(_alibi_slopes(MOBA_HEADS) * LOG2E), qk, vq, q_col0=0, k_col0=mw // hd, v_col0=iw // hd)

    ki = kiw[:, :, :IDX_DIM].astype(BF16)
    wi = kiw[:, :, IDX_DIM:IDX_DIM + IDX_HEADS]
    bias = _dsa_select(vq, 0, ki, wi)
    o_b = _dsa_attn(_bf16_parts(_alibi_slopes(DSA_HEADS) * LOG2E), qk, vq, bias, q_col0=2 * mw // hd,
                    k_col0=(2 * mw + qw) // hd, v_col0=(iw + mw) // hd)

    merged = _merge(o_a, w_proj_moba.astype(BF16), o_b, w_proj_dsa.astype(BF16), gates)
    x1 = _resid_mm(merged, w_out.astype(BF16), x, g1)

    h2, h2_packed = _normmod(x1, ln2_gain, sc2, sh2, with_packed=True)
    wr_t = w_router.T.astype(BF16)
    idx_t, wgt_t = _router(h2.reshape(t, d), wr_t, router_bias.reshape(-1, 1))
    e_chunk, n_active, tok_pad, dst_pad, n_assign = _dispatch(idx_t, d // 2 // LANES)
    yk = _moe(e_chunk, n_active, tok_pad, dst_pad, h2_packed,
              w1_experts.astype(BF16), w3_experts.astype(BF16), w2_experts.astype(BF16), n_assign)
    hs = _swiglu_up(h2, w1_shared.astype(BF16), w3_shared.astype(BF16))
    return _final(hs, w2_shared.astype(BF16), yk, wgt_t.T, x1, g2)


def kernel(x, c, ln1_gain, ln2_gain, w_ada, b_ada, w_in, moba_q_gain, moba_k_gain, dsa_q_gain, dsa_k_gain, idx_k_gain, w_proj_moba, w_proj_dsa, w_out, w_router, router_bias, w1_experts, w3_experts, w2_experts, w1_shared, w3_shared, w2_shared):
    depth = w_ada.shape[0]
    b = x.shape[0]
    c_pad = jnp.pad(c, ((0, (-b) % 8), (0, 0)))
    params = (ln1_gain, ln2_gain, w_ada, b_ada, w_in, moba_q_gain, moba_k_gain, dsa_q_gain, dsa_k_gain,
              idx_k_gain, w_proj_moba, w_proj_dsa, w_out, w_router, router_bias, w1_experts, w3_experts,
              w2_experts, w1_shared, w3_shared, w2_shared)
    for l in range(depth):
        x = _layer(x, c_pad, *[p[l] for p in params])
    return x
```

```python
import functools

import jax
import jax.numpy as jnp
from jax import lax
from jax.experimental import pallas as pl
from jax.experimental.pallas import tpu as pltpu

F32 = jnp.float32
BF16 = jnp.bfloat16
I32 = jnp.int32
U32 = jnp.uint32

HEAD_DIM = 128
MOBA_HEADS = 16
MOBA_BLOCK = 256
MOBA_TOPK = 3
DSA_HEADS = 16
DSA_KV_HEADS = 4
DSA_TOPK_MAX = 256
IDX_HEADS = 32
IDX_DIM = 64
N_EXPERTS = 64
EXPERT_TOPK = 8
N_GROUPS = 8
TOPK_GROUPS = 4
ROUTED_SCALE = 2.5
NORM_EPS = 1e-6

LANES = 128
V7X_VMEM_LIMIT_BYTES = 56 * 1024 * 1024

MOE_CHUNK = 256
YK_PACK_TILE = 512
NEG = -1e30
INT_MIN = -2 ** 31
LOG2E = 1.4426950408889634

_NT = (((1,), (1,)), ((), ()))


def _cparams(sem):
    return pltpu.CompilerParams(dimension_semantics=sem, vmem_limit_bytes=V7X_VMEM_LIMIT_BYTES)


def _tile(n, pref):
    if n <= pref:
        return n
    t = pref - pref % LANES
    while t > LANES and n % t:
        t -= LANES
    assert n % t == 0, (n, pref)
    return t


def _ada_kernel(c_ref, w_ref, b_ref, o_ref):
    c = c_ref[...]
    s = c * jax.nn.sigmoid(c)
    acc = jnp.dot(s.astype(BF16), w_ref[...].astype(BF16), preferred_element_type=F32)
    o_ref[...] = acc + b_ref[...]


def _ada(c_pad, w_ada, b_ada):
    rows, d = c_pad.shape
    n = w_ada.shape[1]
    tn = _tile(n, 512)
    return pl.pallas_call(
        _ada_kernel,
        grid=(n // tn,),
        in_specs=[pl.BlockSpec((rows, d), lambda j: (0, 0)),
                  pl.BlockSpec((d, tn), lambda j: (0, j)),
                  pl.BlockSpec((1, tn), lambda j: (0, j))],
        out_specs=pl.BlockSpec((rows, tn), lambda j: (0, j)),
        out_shape=jax.ShapeDtypeStruct((rows, n), F32),
        compiler_params=_cparams(("arbitrary",)),
        name="ada_mod",
    )(c_pad, w_ada, b_ada)


def _pack_bf16_pairs(v):
    n = v.shape[1] // 2
    bits = lax.bitcast_convert_type(v.astype(F32), U32)
    return (bits[:, :n] >> 16) | bits[:, n:]


def _unpack_bf16_pairs(w):
    lo = lax.bitcast_convert_type(w << 16, F32)
    hi = lax.bitcast_convert_type(w & jnp.uint32(0xFFFF0000), F32)
    return lo, hi


def _normmod_kernel(x_ref, gain_ref, sc_ref, sh_ref, *o_refs):
    x = x_ref[0]
    y = x * lax.rsqrt(jnp.mean(x * x, axis=-1, keepdims=True) + NORM_EPS) * gain_ref[...]
    h = (y * (1.0 + sc_ref[0]) + sh_ref[0]).astype(BF16)
    o_refs[0][0] = h
    if len(o_refs) > 1:
        words = _pack_bf16_pairs(h)
        ts = words.shape[0]
        slab_rows = words.shape[1] // LANES
        for a in range(slab_rows):
            o_refs[1][pl.ds(a, ts, stride=slab_rows), :] = words[:, a * LANES:(a + 1) * LANES]


def _normmod(x, gain, sc, sh, with_packed):
    b, s, d = x.shape
    ts = _tile(s, 256)
    blk = pl.BlockSpec((1, ts, d), lambda bi, i: (bi, i, 0))
    vec = pl.BlockSpec((1, 1, d), lambda bi, i: (bi, 0, 0))
    out_shape = [jax.ShapeDtypeStruct((b, s, d), BF16)]
    out_specs = [blk]
    if with_packed:
        slab_rows = d // 2 // LANES
        out_shape.append(jax.ShapeDtypeStruct((b * s * slab_rows, LANES), U32))
        out_specs.append(pl.BlockSpec((ts * slab_rows, LANES), lambda bi, i: (bi * (s // ts) + i, 0)))
    return pl.pallas_call(
        _normmod_kernel,
        grid=(b, s // ts),
        in_specs=[blk, pl.BlockSpec((1, d), lambda bi, i: (0, 0)), vec, vec],
        out_specs=out_specs,
        out_shape=out_shape,
        compiler_params=_cparams(("arbitrary", "arbitrary")),
        name="norm_modulate",
    )(x, gain.reshape(1, d), sc.reshape(b, 1, d), sh.reshape(b, 1, d))


def _proj_kernel(a_ref, w_ref, aux_ref, o_ref, *, mode):
    acc = jnp.dot(a_ref[0], w_ref[...], preferred_element_type=F32)
    tn = acc.shape[1]
    if mode == "plain":
        o_ref[0] = acc.astype(o_ref.dtype)
    elif mode == "sigmoid":
        o_ref[0] = jax.nn.sigmoid(acc).astype(o_ref.dtype)
    elif mode == "headnorm":
        for g in range(tn // HEAD_DIM):
            sl = slice(g * HEAD_DIM, (g + 1) * HEAD_DIM)
            blk = acc[:, sl]
            ms = jnp.mean(blk * blk, axis=-1, keepdims=True)
            o_ref[0, :, sl] = (blk * lax.rsqrt(ms + NORM_EPS) * aux_ref[:, sl]).astype(o_ref.dtype)
    elif mode == "indexer":
        lane = lax.broadcasted_iota(I32, acc.shape, 1)
        is_key = lane < IDX_DIM
        ms = jnp.sum(jnp.where(is_key, acc * acc, 0.0), axis=-1, keepdims=True) / IDX_DIM
        o_ref[0] = acc * jnp.where(is_key, lax.rsqrt(ms + NORM_EPS), 1.0) * aux_ref[...]
    else:
        raise ValueError(mode)


def _proj(a, w, aux, mode, out_dtype, tm_pref=1024, tn_pref=512):
    b, s, k = a.shape
    n = w.shape[1]
    tm, tn = _tile(s, tm_pref), _tile(n, tn_pref)
    return pl.pallas_call(
        functools.partial(_proj_kernel, mode=mode),
        grid=(b, s // tm, n // tn),
        in_specs=[pl.BlockSpec((1, tm, k), lambda bi, i, j: (bi, i, 0)),
                  pl.BlockSpec((k, tn), lambda bi, i, j: (0, j)),
                  pl.BlockSpec((1, tn), lambda bi, i, j: (0, j))],
        out_specs=pl.BlockSpec((1, tm, tn), lambda bi, i, j: (bi, i, j)),
        out_shape=jax.ShapeDtypeStruct((b, s, n), out_dtype),
        compiler_params=_cparams(("arbitrary", "arbitrary", "arbitrary")),
        name="proj_" + mode,
    )(a, w, aux)


AUG_ALIBI = 0
AUG_SEL = 8
MOBA_QBLOCKS = 4


def _moba_kernel(sp_ref, q_ref, k_ref, v_ref, o_ref, kaug_sc, vt_sc, kmean_sc, *, nb, gt, qb):
    h = pl.program_id(1)
    ip = pl.program_id(2)
    blk = MOBA_BLOCK
    tq = qb * blk
    nbp = kmean_sc.shape[0]
    gblocks = gt // blk

    @pl.when(ip == 0)
    def _():
        kmean_sc[...] = jnp.zeros_like(kmean_sc)
        lane = lax.broadcasted_iota(I32, (blk, LANES), 1)
        row = lax.broadcasted_iota(I32, (blk, LANES), 0)
        for n in range(nb):
            rows = slice(n * blk, (n + 1) * blk)
            kb = k_ref[0, rows, :]
            kmean_sc[n:n + 1, :] = jnp.mean(kb.astype(F32), axis=0, keepdims=True)
            aug = jnp.where(lane == AUG_SEL + n, 1.0, _alibi_key_aug(n * blk + row, lane))
            kaug_sc[rows, :HEAD_DIM] = kb
            kaug_sc[rows, HEAD_DIM:] = aug.astype(BF16)
            vt_sc[:, rows] = v_ref[0, rows, :].astype(F32).T.astype(BF16)

    q = q_ref[0]
    gate_t = lax.dot_general(kmean_sc[...], q.astype(F32), _NT, preferred_element_type=F32,
                             precision=lax.Precision.HIGHEST)
    r_iota = lax.broadcasted_iota(I32, gate_t.shape, 0)
    assert blk & (blk - 1) == 0
    own = ip * qb + (lax.broadcasted_iota(I32, gate_t.shape, 1) >> (blk.bit_length() - 1))
    rank = jnp.zeros(gate_t.shape, I32)
    for m in range(nb):
        gm = gate_t[m:m + 1, :]
        beats = ((gm > gate_t) | ((gm == gate_t) & (m < r_iota))) & (m < own)
        rank = rank + jnp.where(beats, 1, 0)
    keep_t = ((r_iota < own) & (rank < MOBA_TOPK)) | (r_iota == own)
    selb_t = jnp.where(keep_t, 0.0, NEG)

    augq_t = jnp.concatenate([_alibi_query_aug(sp_ref, h, (8, tq), 0), selb_t,
                              jnp.zeros((LANES - 8 - nbp, tq), F32)], axis=0)
    qa = jnp.concatenate([q, augq_t.T.astype(BF16)], axis=1)

    t_pos = ip * tq + lax.broadcasted_iota(I32, (gt, tq), 1)
    s_off = lax.broadcasted_iota(I32, (gt, tq), 0)

    def group(gi, carry, causal):
        m_prev, l_prev, acc_prev = carry
        g0 = pl.multiple_of(gi * gt, gt)
        st = lax.dot_general(kaug_sc[pl.ds(g0, gt), :], qa, _NT, preferred_element_type=F32)
        if causal:
            st = jnp.where(g0 + s_off <= t_pos, st, NEG)
        m_new = jnp.maximum(m_prev, jnp.max(st, axis=0, keepdims=True))
        alpha = jnp.exp2(m_prev - m_new)
        p = jnp.exp2(st - m_new)
        l_new = alpha * l_prev + jnp.sum(p, axis=0, keepdims=True)
        pv = jnp.dot(vt_sc[:, pl.ds(g0, gt)], p.astype(BF16), preferred_element_type=F32)
        return m_new, l_new, alpha * acc_prev + pv

    init = (jnp.full((1, tq), NEG, F32), jnp.zeros((1, tq), F32), jnp.zeros((HEAD_DIM, tq), F32))
    n_full = (ip * qb) // gblocks
    carry = lax.fori_loop(0, n_full, lambda gi, c: group(gi, c, False), init)
    _, l_fin, acc_fin = group(n_full, carry, True)
    o_ref[0] = (acc_fin / l_fin).T.astype(o_ref.dtype)


def _moba(slope_parts, qk, vq, q_col0, k_col0, v_col0):
    b, s, _ = qk.shape
    assert s % MOBA_BLOCK == 0 and s <= 4096
    nb = s // MOBA_BLOCK
    nbp = -(-nb // 8) * 8
    assert AUG_SEL + nbp <= LANES
    gt = min(s, 4 * MOBA_BLOCK)
    qb = MOBA_QBLOCKS
    assert s % gt == 0 and nb % qb == 0 and (gt // MOBA_BLOCK) % qb == 0
    tq = qb * MOBA_BLOCK
    return pl.pallas_call(
        functools.partial(_moba_kernel, nb=nb, gt=gt, qb=qb),
        grid=(b, MOBA_HEADS, nb // qb),
        in_specs=[pl.BlockSpec(memory_space=pltpu.SMEM),
                  pl.BlockSpec((1, tq, HEAD_DIM), lambda bi, h, i: (bi, i, q_col0 + h)),
                  pl.BlockSpec((1, s, HEAD_DIM), lambda bi, h, i: (bi, 0, k_col0 + h)),
                  pl.BlockSpec((1, s, HEAD_DIM), lambda bi, h, i: (bi, 0, v_col0 + h))],
        out_specs=pl.BlockSpec((1, tq, HEAD_DIM), lambda bi, h, i: (bi, i, h)),
        out_shape=jax.ShapeDtypeStruct((b, s, MOBA_HEADS * HEAD_DIM), BF16),
        scratch_shapes=[pltpu.VMEM((s, 2 * HEAD_DIM), BF16),
                        pltpu.VMEM((HEAD_DIM, s), BF16),
                        pltpu.VMEM((nbp, HEAD_DIM), F32)],
        compiler_params=_cparams(("arbitrary", "arbitrary", "arbitrary")),
        name="moba_attention",
    )(slope_parts, qk, qk, vq)


def _dsa_select_kernel(qi_ref, ki_ref, wi_ref, bias_ref, key_sc, qs_sc, *, tq, tk, nsel, hgroup):
    j = pl.program_id(1)
    t0 = j * tq
    nkc = (t0 + tq + tk - 1) // tk
    n_hg = IDX_HEADS // hgroup

    bias_ref[...] = jnp.full(bias_ref.shape, NEG, bias_ref.dtype)
    for hd in range(IDX_HEADS):
        qs_sc[hd * tq:(hd + 1) * tq, :] = qi_ref[0, :, hd * IDX_DIM:(hd + 1) * IDX_DIM]
    w = wi_ref[0]
    row_t = t0 + lax.broadcasted_iota(I32, (tq, tk), 0)
    col_i = lax.broadcasted_iota(I32, (tq, tk), 1)

    def score_chunk(c, _):
        c0 = pl.multiple_of(c * tk, tk)
        kc = ki_ref[0, pl.ds(c0, tk), :]
        acc = jnp.zeros((tq, tk), F32)
        for g in range(n_hg):
            qg = qs_sc[g * hgroup * tq:(g + 1) * hgroup * tq, :]
            d = lax.dot_general(qg, kc, _NT, preferred_element_type=F32)
            for hh in range(hgroup):
                hd = g * hgroup + hh
                acc = acc + w[:, hd:hd + 1] * jnp.maximum(d[hh * tq:(hh + 1) * tq], 0.0)
        bits = lax.bitcast_convert_type(acc, I32)
        key = jnp.where(bits >= 0, bits, bits ^ 0x7FFFFFFF)
        key = jnp.where(c0 + col_i <= row_t, key, INT_MIN)
        key_sc[:, pl.ds(c0, tk)] = key
        return 0

    lax.fori_loop(0, nkc, score_chunk, 0)

    def bit_body(bi, prefix):
        cand = prefix + jnp.left_shift(jnp.int32(1), 31 - bi)

        def cnt_chunk(c, cnt):
            blk = key_sc[:, pl.ds(pl.multiple_of(c * tk, tk), tk)]
            ge = jnp.where(blk >= cand, 1.0, 0.0)
            for g in range(tk // LANES):
                cnt = cnt + ge[:, g * LANES:(g + 1) * LANES]
            return cnt

        cnt = lax.fori_loop(0, nkc, cnt_chunk, jnp.zeros((tq, LANES), F32))
        total = jnp.sum(cnt, axis=-1, keepdims=True)
        return jnp.where(total >= nsel, cand, prefix)

    thr = lax.fori_loop(0, 32, bit_body, jnp.full((tq, 1), INT_MIN, I32))

    def out_chunk(c, _):
        c0 = pl.multiple_of(c * tk, tk)
        blk = key_sc[:, pl.ds(c0, tk)]
        keep = (blk >= thr) & (c0 + col_i <= row_t)
        bias_ref[0, :, pl.ds(c0, tk)] = jnp.where(keep, 0.0, NEG).astype(bias_ref.dtype)
        return 0

    lax.fori_loop(0, nkc, out_chunk, 0)


def _dsa_select(vq, qi_blk, ki, wi):
    b, s, _ = vq.shape
    iw = IDX_HEADS * IDX_DIM
    tq = _tile(s, 128)
    tk = _tile(s, 512)
    nsel = min(DSA_TOPK_MAX, s // 4)
    return pl.pallas_call(
        functools.partial(_dsa_select_kernel, tq=tq, tk=tk, nsel=nsel, hgroup=8),
        grid=(b, s // tq),
        in_specs=[pl.BlockSpec((1, tq, iw), lambda bi, j: (bi, j, qi_blk)),
                  pl.BlockSpec((1, s, IDX_DIM), lambda bi, j: (bi, 0, 0)),
                  pl.BlockSpec((1, tq, IDX_HEADS), lambda bi, j: (bi, j, 0))],
        out_specs=pl.BlockSpec((1, tq, s), lambda bi, j: (bi, j, 0)),
        out_shape=jax.ShapeDtypeStruct((b, s, s), BF16),
        scratch_shapes=[pltpu.VMEM((tq, s), I32), pltpu.VMEM((IDX_HEADS * tq, IDX_DIM), BF16)],
        compiler_params=_cparams(("arbitrary", "arbitrary")),
        name="dsa_select",
    )(vq, ki, wi)


def _alibi_key_aug(pos, lane):
    alibi = jnp.where((lane & 1) == 0, ((pos >> 4) << 4).astype(F32), (pos & 15).astype(F32))
    return jnp.where(lane < AUG_ALIBI + 6, alibi, 0.0)


def _alibi_query_aug(sp_ref, h, shape, axis):
    ix = lax.broadcasted_iota(I32, shape, axis)
    return jnp.where(ix < 2, sp_ref[h, 0], jnp.where(ix < 4, sp_ref[h, 1], jnp.where(ix < 6, sp_ref[h, 2], 0.0)))


def _dsa_attn_kernel(sp_ref, q_ref, k_ref, v_ref, bias_ref, o_ref, kaug_sc, *, tq, tk, grp):
    g = pl.program_id(1)
    j = pl.program_id(2)
    t0 = j * tq
    nk = (t0 + tq + tk - 1) // tk
    rows = grp * tq
    dh = HEAD_DIM
    s_len = kaug_sc.shape[0]

    @pl.when(j == 0)
    def _():
        lane = lax.broadcasted_iota(I32, (tk, LANES), 1)
        row = lax.broadcasted_iota(I32, (tk, LANES), 0)
        for n in range(s_len // tk):
            rs = slice(n * tk, (n + 1) * tk)
            kaug_sc[rs, :dh] = k_ref[0, rs, :]
            kaug_sc[rs, dh:] = _alibi_key_aug(n * tk + row, lane).astype(BF16)

    q = jnp.concatenate(
        [jnp.concatenate([q_ref[0, :, r * dh:(r + 1) * dh],
                          _alibi_query_aug(sp_ref, g * grp + r, (tq, LANES), 1).astype(BF16)], axis=1)
         for r in range(grp)], axis=0)

    def body(c, carry):
        m_prev, l_prev, acc_prev = carry
        c0 = pl.multiple_of(c * tk, tk)
        s = lax.dot_general(q, kaug_sc[pl.ds(c0, tk), :], _NT, preferred_element_type=F32)
        mb = bias_ref[0, :, pl.ds(c0, tk)].astype(F32)
        lg = jnp.concatenate([s[r * tq:(r + 1) * tq] + mb for r in range(grp)], axis=0)
        m_new = jnp.maximum(m_prev, jnp.max(lg, axis=-1, keepdims=True))
        alpha = jnp.exp2(m_prev - m_new)
        p = jnp.exp2(lg - m_new)
        l_new = alpha * l_prev + jnp.sum(p, axis=-1, keepdims=True)
        pv = jnp.dot(p.astype(BF16), v_ref[0, pl.ds(c0, tk), :], preferred_element_type=F32)
        return m_new, l_new, alpha * acc_prev + pv

    init = (jnp.full((rows, 1), NEG, F32), jnp.zeros((rows, 1), F32), jnp.zeros((rows, dh), F32))
    _, l_fin, acc_fin = lax.fori_loop(0, nk, body, init)
    out = acc_fin / l_fin
    for r in range(grp):
        o_ref[0, :, r * dh:(r + 1) * dh] = out[r * tq:(r + 1) * tq].astype(o_ref.dtype)


def _dsa_attn(slope_parts, qk, vq, bias, q_col0, k_col0, v_col0):
    b, s, _ = qk.shape
    assert s <= 4096
    grp = DSA_HEADS // DSA_KV_HEADS
    gw = grp * HEAD_DIM
    assert (q_col0 * HEAD_DIM) % gw == 0
    q_blk0 = q_col0 * HEAD_DIM // gw
    tq = _tile(s, 256)
    tk = _tile(s, 512)
    return pl.pallas_call(
        functools.partial(_dsa_attn_kernel, tq=tq, tk=tk, grp=grp),
        grid=(b, DSA_KV_HEADS, s // tq),
        in_specs=[pl.BlockSpec(memory_space=pltpu.SMEM),
                  pl.BlockSpec((1, tq, gw), lambda bi, g, j: (bi, j, q_blk0 + g)),
                  pl.BlockSpec((1, s, HEAD_DIM), lambda bi, g, j: (bi, 0, k_col0 + g)),
                  pl.BlockSpec((1, s, HEAD_DIM), lambda bi, g, j: (bi, 0, v_col0 + g)),
                  pl.BlockSpec((1, tq, s), lambda bi, g, j: (bi, j, 0))],
        out_specs=pl.BlockSpec((1, tq, gw), lambda bi, g, j: (bi, j, g)),
        out_shape=jax.ShapeDtypeStruct((b, s, DSA_HEADS * HEAD_DIM), BF16),
        scratch_shapes=[pltpu.VMEM((s, 2 * HEAD_DIM), BF16)],
        compiler_params=_cparams(("arbitrary", "arbitrary", "arbitrary")),
        name="dsa_attention",
    )(slope_parts, qk, qk, vq, bias)


def _merge_kernel(oa_ref, wa_ref, ob_ref, wb_ref, ga_ref, gb_ref, o_ref):
    ya = jnp.dot(oa_ref[0], wa_ref[...], preferred_element_type=F32)
    yb = jnp.dot(ob_ref[0], wb_ref[...], preferred_element_type=F32)
    o_ref[0] = (ga_ref[0].astype(F32) * ya + gb_ref[0].astype(F32) * yb).astype(o_ref.dtype)


def _merge(oa, wa, ob, wb, gates):
    b, s, ka = oa.shape
    kb = ob.shape[2]
    n = wa.shape[1]
    tm, tn = _tile(s, 1024), _tile(n, 512)
    nj = n // tn
    return pl.pallas_call(
        _merge_kernel,
        grid=(b, s // tm, nj),
        in_specs=[pl.BlockSpec((1, tm, ka), lambda bi, i, j: (bi, i, 0)),
                  pl.BlockSpec((ka, tn), lambda bi, i, j: (0, j)),
                  pl.BlockSpec((1, tm, kb), lambda bi, i, j: (bi, i, 0)),
                  pl.BlockSpec((kb, tn), lambda bi, i, j: (0, j)),
                  pl.BlockSpec((1, tm, tn), lambda bi, i, j: (bi, i, j)),
                  pl.BlockSpec((1, tm, tn), lambda bi, i, j: (bi, i, nj + j))],
        out_specs=pl.BlockSpec((1, tm, tn), lambda bi, i, j: (bi, i, j)),
        out_shape=jax.ShapeDtypeStruct((b, s, n), BF16),
        compiler_params=_cparams(("arbitrary", "arbitrary", "arbitrary")),
        name="gated_merge",
    )(oa, wa, ob, wb, gates, gates)


def _resid_kernel(a_ref, w_ref, x_ref, g_ref, o_ref):
    y = jnp.dot(a_ref[0], w_ref[...], preferred_element_type=F32)
    o_ref[0] = x_ref[0] + g_ref[0] * y


def _resid_mm(a, w, x, g):
    b, s, k = a.shape
    n = w.shape[1]
    tm, tn = _tile(s, 1024), _tile(n, 512)
    return pl.pallas_call(
        _resid_kernel,
        grid=(b, s // tm, n // tn),
        in_specs=[pl.BlockSpec((1, tm, k), lambda bi, i, j: (bi, i, 0)),
                  pl.BlockSpec((k, tn), lambda bi, i, j: (0, j)),
                  pl.BlockSpec((1, tm, tn), lambda bi, i, j: (bi, i, j)),
                  pl.BlockSpec((1, 1, tn), lambda bi, i, j: (bi, 0, j))],
        out_specs=pl.BlockSpec((1, tm, tn), lambda bi, i, j: (bi, i, j)),
        out_shape=jax.ShapeDtypeStruct((b, s, n), F32),
        compiler_params=_cparams(("arbitrary", "arbitrary", "arbitrary")),
        name="out_proj_residual",
    )(a, w, x, g.reshape(b, 1, n))


def _swiglu_kernel(a_ref, w1_ref, w3_ref, o_ref):
    a = a_ref[0]
    u = jnp.dot(a, w1_ref[...], preferred_element_type=F32)
    v = jnp.dot(a, w3_ref[...], preferred_element_type=F32)
    o_ref[0] = (u * jax.nn.sigmoid(u) * v).astype(o_ref.dtype)


def _swiglu_up(a, w1, w3):
    b, s, k = a.shape
    n = w1.shape[1]
    tm, tn = _tile(s, 1024), _tile(n, 512)
    return pl.pallas_call(
        _swiglu_kernel,
        grid=(b, s // tm, n // tn),
        in_specs=[pl.BlockSpec((1, tm, k), lambda bi, i, j: (bi, i, 0)),
                  pl.BlockSpec((k, tn), lambda bi, i, j: (0, j)),
                  pl.BlockSpec((k, tn), lambda bi, i, j: (0, j))],
        out_specs=pl.BlockSpec((1, tm, tn), lambda bi, i, j: (bi, i, j)),
        out_shape=jax.ShapeDtypeStruct((b, s, n), BF16),
        compiler_params=_cparams(("arbitrary", "arbitrary", "arbitrary")),
        name="shared_swiglu_up",
    )(a, w1, w3)


def _final_kernel(a_ref, w_ref, *rest):
    yk_refs, (rw_ref, x_ref, g_ref, o_ref) = rest[:EXPERT_TOPK], rest[EXPERT_TOPK:]
    y = jnp.dot(a_ref[0], w_ref[...], preferred_element_type=F32)
    lo = hi = None
    for kk, r in enumerate(yk_refs):
        lo_r, hi_r = _unpack_bf16_pairs(r[...])
        wk = rw_ref[:, kk:kk + 1]
        lo = lo_r * wk if lo is None else lo + lo_r * wk
        hi = hi_r * wk if hi is None else hi + hi_r * wk
    o_ref[0] = x_ref[0] + g_ref[0] * (y + jnp.concatenate([lo, hi], axis=1))


def _final(hs, w2, yk, route_w, x, g):
    b, s, k = hs.shape
    n = w2.shape[1]
    tm, tn = _tile(s, 512), YK_PACK_TILE
    assert n % tn == 0
    nti = s // tm
    t_blocks = b * nti
    yk_specs = [pl.BlockSpec((tm, tn // 2), lambda bi, i, j, kk=kk: (kk * t_blocks + bi * nti + i, j))
                for kk in range(EXPERT_TOPK)]
    return pl.pallas_call(
        _final_kernel,
        grid=(b, nti, n // tn),
        in_specs=[pl.BlockSpec((1, tm, k), lambda bi, i, j: (bi, i, 0)),
                  pl.BlockSpec((k, tn), lambda bi, i, j: (0, j))] + yk_specs + [
                  pl.BlockSpec((tm, EXPERT_TOPK), lambda bi, i, j: (bi * nti + i, 0)),
                  pl.BlockSpec((1, tm, tn), lambda bi, i, j: (bi, i, j)),
                  pl.BlockSpec((1, 1, tn), lambda bi, i, j: (bi, 0, j))],
        out_specs=pl.BlockSpec((1, tm, tn), lambda bi, i, j: (bi, i, j)),
        out_shape=jax.ShapeDtypeStruct((b, s, n), F32),
        compiler_params=_cparams(("arbitrary", "arbitrary", "arbitrary")),
        name="moe_combine_residual",
    )(hs, w2, *([yk] * EXPERT_TOPK), route_w, x, g.reshape(b, 1, n))


def _router_kernel(h_ref, wr_ref, b_ref, idx_ref, wgt_ref):
    tm = h_ref.shape[0]
    per = N_EXPERTS // N_GROUPS
    logits = lax.dot_general(wr_ref[...], h_ref[...], _NT, preferred_element_type=F32)
    scores = jax.nn.sigmoid(logits)
    choice = scores + b_ref[...]
    sub = lax.broadcasted_iota(I32, (per, tm), 0).astype(F32)
    gs = []
    for g in range(N_GROUPS):
        cg = choice[g * per:(g + 1) * per]
        m1 = jnp.max(cg, axis=0, keepdims=True)
        i1 = jnp.min(jnp.where(cg == m1, sub, float(per)), axis=0, keepdims=True)
        m2 = jnp.max(jnp.where(sub == i1, -jnp.inf, cg), axis=0, keepdims=True)
        gs.append(m1 + m2)
    masked = []
    for g in range(N_GROUPS):
        rank = jnp.zeros((1, tm), F32)
        for g2 in range(N_GROUPS):
            if g2 == g:
                continue
            beats = (gs[g2] > gs[g]) | ((gs[g2] == gs[g]) & (g2 < g))
            rank = rank + jnp.where(beats, 1.0, 0.0)
        masked.append(jnp.where(rank < TOPK_GROUPS, choice[g * per:(g + 1) * per], -jnp.inf))
    cm = jnp.concatenate(masked, axis=0)
    erow = lax.broadcasted_iota(I32, (N_EXPERTS, tm), 0).astype(F32)
    idxs, wts = [], []
    for _ in range(EXPERT_TOPK):
        mx = jnp.max(cm, axis=0, keepdims=True)
        ix = jnp.min(jnp.where(cm == mx, erow, float(N_EXPERTS)), axis=0, keepdims=True)
        hit = erow == ix
        idxs.append(ix)
        wts.append(jnp.sum(jnp.where(hit, scores, 0.0), axis=0, keepdims=True))
        cm = jnp.where(hit, -jnp.inf, cm)
    wt = jnp.concatenate(wts, axis=0)
    idx_ref[...] = jnp.concatenate(idxs, axis=0).astype(I32)
    wgt_ref[...] = wt / jnp.sum(wt, axis=0, keepdims=True) * ROUTED_SCALE


def _router(h2_flat, wr_t, bias_col):
    t, d = h2_flat.shape
    tm = _tile(t, 512)
    return pl.pallas_call(
        _router_kernel,
        grid=(t // tm,),
        in_specs=[pl.BlockSpec((tm, d), lambda i: (i, 0)),
                  pl.BlockSpec((N_EXPERTS, d), lambda i: (0, 0)),
                  pl.BlockSpec((N_EXPERTS, 1), lambda i: (0, 0))],
        out_specs=[pl.BlockSpec((EXPERT_TOPK, tm), lambda i: (0, i)),
                   pl.BlockSpec((EXPERT_TOPK, tm), lambda i: (0, i))],
        out_shape=[jax.ShapeDtypeStruct((EXPERT_TOPK, t), I32),
                   jax.ShapeDtypeStruct((EXPERT_TOPK, t), F32)],
        compiler_params=_cparams(("arbitrary",)),
        name="router_topk",
    )(h2_flat, wr_t, bias_col)


MOE_DMA_UNROLL = 8
DMA_PRIORITIES = 2


def _moe_kernel(ech_ref, nact_ref, tokc_ref, tokn_ref, dst_ref, h_hbm, w1_ref, w3_ref, w2_ref,
                yk_hbm, xs, ys, gsem, ssem, *, dump_row0):
    del ech_ref
    c = pl.program_id(0)
    nc = pl.num_programs(0)
    nact = nact_ref[0]
    rows = MOE_CHUNK
    sl = xs.shape[1] // rows
    slot = c % 2

    def gather_copy(tok_ref, r, s):
        return pltpu.make_async_copy(h_hbm.at[pl.ds(pl.multiple_of(tok_ref[0, 0, r], sl), sl), :],
                                     xs.at[s, pl.ds(pl.multiple_of(r * sl, sl), sl), :], gsem.at[s])

    def scatter_copy(r, dst_row):
        return pltpu.make_async_copy(ys.at[pl.ds(r, 1), :], yk_hbm.at[pl.ds(dst_row, 1), :], ssem.at[0])

    def per_row(fn):
        def body(g, carry):
            for j in range(MOE_DMA_UNROLL):
                fn(g * MOE_DMA_UNROLL + j, j % DMA_PRIORITIES)
            return carry
        lax.fori_loop(0, rows // MOE_DMA_UNROLL, body, 0)

    def wait_scatter():
        per_row(lambda r, p: scatter_copy(r, 0).wait())

    @pl.when(c == 0)
    def _():
        ys[...] = jnp.zeros_like(ys)
        per_row(lambda r, p: scatter_copy(r, dump_row0 + r).start(priority=p))
        per_row(lambda r, p: gather_copy(tokc_ref, r, 0).start(priority=p))

    @pl.when(c < nact)
    def _():
        per_row(lambda r, p: gather_copy(tokn_ref, r, 1 - slot).start(priority=p))
        per_row(lambda r, p: gather_copy(tokc_ref, r, slot).wait())

        xsl = xs.at[slot]
        parts = [_unpack_bf16_pairs(xsl[pl.ds(a, rows, stride=sl), :]) for a in range(sl)]
        x = jnp.concatenate([p[0] for p in parts] + [p[1] for p in parts], axis=1).astype(BF16)
        u = jnp.dot(x, w1_ref[0], preferred_element_type=F32)
        v = jnp.dot(x, w3_ref[0], preferred_element_type=F32)
        hm = (u * jax.nn.sigmoid(u) * v).astype(BF16)
        d = w2_ref.shape[2]
        half = YK_PACK_TILE // 2
        tiles = []
        for j in range(d // YK_PACK_TILE):
            yj = jnp.dot(hm, w2_ref[0, :, j * YK_PACK_TILE:(j + 1) * YK_PACK_TILE],
                         preferred_element_type=F32)
            tiles.append(_pack_bf16_pairs(yj.astype(BF16)))
        wait_scatter()
        for j, tile in enumerate(tiles):
            ys[:, j * half:(j + 1) * half] = tile
        per_row(lambda r, p: scatter_copy(r, dst_ref[0, 0, r]).start(priority=p))

    @pl.when(c == nc - 1)
    def _():
        wait_scatter()
        per_row(lambda r, p: gather_copy(tokc_ref, r, nact % 2).wait())


def _moe(e_chunk, n_active, tok_pad, dst_pad, h2_slabs, w1, w3, w2, n_assign):
    n_chunks = tok_pad.shape[0]
    d, f = w1.shape[1], w1.shape[2]
    dp = d // 2
    sl = dp // LANES
    assert dp % LANES == 0 and d % YK_PACK_TILE == 0 and h2_slabs.shape[1] == LANES
    rows = MOE_CHUNK
    smem_blk = lambda fn: pl.BlockSpec((1, 1, rows), fn, memory_space=pltpu.SMEM)
    grid_spec = pltpu.PrefetchScalarGridSpec(
        num_scalar_prefetch=2,
        grid=(n_chunks,),
        in_specs=[smem_blk(lambda c, ech, na: (c, 0, 0)),
                  smem_blk(lambda c, ech, na: (jnp.minimum(c + 1, n_chunks - 1), 0, 0)),
                  smem_blk(lambda c, ech, na: (c, 0, 0)),
                  pl.BlockSpec(memory_space=pl.ANY),
                  pl.BlockSpec((1, d, f), lambda c, ech, na: (ech[c], 0, 0)),
                  pl.BlockSpec((1, d, f), lambda c, ech, na: (ech[c], 0, 0)),
                  pl.BlockSpec((1, f, d), lambda c, ech, na: (ech[c], 0, 0))],
        out_specs=pl.BlockSpec(memory_space=pl.ANY),
        scratch_shapes=[pltpu.VMEM((2, rows * sl, LANES), U32),
                        pltpu.VMEM((rows, dp), U32),
                        pltpu.SemaphoreType.DMA((2,)),
                        pltpu.SemaphoreType.DMA((1,))],
    )
    return pl.pallas_call(
        functools.partial(_moe_kernel, dump_row0=n_assign),
        grid_spec=grid_spec,
        out_shape=jax.ShapeDtypeStruct((n_assign + rows, dp), U32),
        compiler_params=_cparams(("arbitrary",)),
        name="routed_experts",
    )(e_chunk, n_active, tok_pad, tok_pad, dst_pad, h2_slabs, w1, w3, w2)


def _dispatch(idx_t, slab_rows):
    k, t = idx_t.shape
    e, c = N_EXPERTS, MOE_CHUNK
    m = k * t
    n_chunks = m // c + e
    e_flat = idx_t.reshape(m)
    order = jnp.argsort(e_flat).astype(I32)
    counts = jnp.sum((e_flat[None, :] == jnp.arange(e, dtype=I32)[:, None]).astype(I32), axis=1)
    starts = jnp.cumsum(counts) - counts
    padded = (counts + c - 1) // c * c
    pends = jnp.cumsum(padded)
    pstarts = pends - padded
    c_start = jnp.arange(n_chunks, dtype=I32) * c
    e_chunk = jnp.minimum(jnp.sum((c_start[:, None] >= pends[None, :]).astype(I32), axis=1), e - 1)
    r = jnp.arange(c, dtype=I32)[None, :]
    off = (c_start - pstarts[e_chunk])[:, None] + r
    valid = (off < counts[e_chunk][:, None]) & (c_start < pends[-1])[:, None]
    a_p = order[jnp.where(valid, starts[e_chunk][:, None] + off, 0)]
    tok_pad = (jnp.where(valid, a_p % t, 0) * slab_rows).astype(I32)
    dst_pad = jnp.where(valid, a_p, m + r).astype(I32)
    n_active = (pends[-1] // c).astype(I32).reshape(1)
    return e_chunk.astype(I32), n_active, tok_pad[:, None, :], dst_pad[:, None, :], m


def _alibi_slopes(n_heads):
    return 2.0 ** (-8.0 * jnp.arange(1, n_heads + 1, dtype=F32) / n_heads)


def _bf16_parts(v):
    p1 = v.astype(BF16).astype(F32)
    p2 = (v - p1).astype(BF16).astype(F32)
    p3 = (v - p1 - p2).astype(BF16).astype(F32)
    return jnp.stack([p1, p2, p3], axis=1)


def _layer(x, c_pad, ln1_gain, ln2_gain, w_ada, b_ada, w_in, moba_q_gain, moba_k_gain, dsa_q_gain,
           dsa_k_gain, idx_k_gain, w_proj_moba, w_proj_dsa, w_out, w_router, router_bias,
           w1_experts, w3_experts, w2_experts, w1_shared, w3_shared, w2_shared):
    b, s, d = x.shape
    t = b * s
    mw = MOBA_HEADS * HEAD_DIM
    qw = DSA_HEADS * HEAD_DIM
    kvw = DSA_KV_HEADS * HEAD_DIM
    iw = IDX_HEADS * IDX_DIM

    mod = _ada(c_pad, w_ada, b_ada.reshape(1, -1))[:b]
    sh1, sc1, g1, sh2, sc2, g2 = jnp.split(mod, 6, axis=-1)

    (h,) = _normmod(x, ln1_gain, sc1, sh1, with_packed=False)

    o_qa, o_ka, o_va = 0, mw, 2 * mw
    o_qb = 3 * mw
    o_kb = o_qb + qw
    o_vb = o_kb + kvw
    o_qi = o_vb + kvw
    o_ki = o_qi + iw
    o_wi = o_ki + IDX_DIM
    o_ga = o_wi + IDX_HEADS
    o_gb = o_ga + d
    cols = lambda a, n: w_in[:, a:a + n]

    scale = HEAD_DIM ** -0.5 * LOG2E
    w_qk = jnp.concatenate([cols(o_qa, mw), cols(o_ka, mw), cols(o_qb, qw), cols(o_kb, kvw)], axis=1).astype(BF16)
    g_qk = jnp.concatenate([jnp.tile(moba_q_gain * scale, MOBA_HEADS), jnp.tile(moba_k_gain, MOBA_HEADS),
                            jnp.tile(dsa_q_gain * scale, DSA_HEADS), jnp.tile(dsa_k_gain, DSA_KV_HEADS)])
    qk = _proj(h, w_qk, g_qk.reshape(1, -1), "headnorm", BF16)
    w_vq = jnp.concatenate([cols(o_qi, iw), cols(o_va, mw), cols(o_vb, kvw)], axis=1).astype(BF16)
    vq = _proj(h, w_vq, jnp.zeros((1, w_vq.shape[1]), F32), "plain", BF16)
    w_gate = cols(o_ga, 2 * d).astype(BF16)
    gates = _proj(h, w_gate, jnp.zeros((1, 2 * d), F32), "sigmoid", BF16)
    w_ix = jnp.pad(cols(o_ki, IDX_DIM + IDX_HEADS), ((0, 0), (0, LANES - IDX_DIM - IDX_HEADS))).astype(BF16)
    aux_ix = jnp.concatenate([idx_k_gain, jnp.full((IDX_HEADS,), IDX_HEADS ** -0.5 * IDX_DIM ** -0.5, F32),
                              jnp.zeros((LANES - IDX_DIM - IDX_HEADS,), F32)])
    kiw = _proj(h, w_ix, aux_ix.reshape(1, -1), "indexer", F32, tn_pref=LANES)

    hd = HEAD_DIM
    o_a = _moba(_bf16_parts(_alibi_slopes(MOBA_HEADS) * LOG2E), qk, vq, q_col0=0, k_col0=mw // hd, v_col0=iw // hd)

    ki = kiw[:, :, :IDX_DIM].astype(BF16)
    wi = kiw[:, :, IDX_DIM:IDX_DIM + IDX_HEADS]
    bias = _dsa_select(vq, 0, ki, wi)
    o_b = _dsa_attn(_bf16_parts(_alibi_slopes(DSA_HEADS) * LOG2E), qk, vq, bias, q_col0=2 * mw // hd,
                    k_col0=(2 * mw + qw) // hd, v_col0=(iw + mw) // hd)

    merged = _merge(o_a, w_proj_moba.astype(BF16), o_b, w_proj_dsa.astype(BF16), gates)
    x1 = _resid_mm(merged, w_out.astype(BF16), x, g1)

    h2, h2_packed = _normmod(x1, ln2_gain, sc2, sh2, with_packed=True)
    wr_t = w_router.T.astype(BF16)
    idx_t, wgt_t = _router(h2.reshape(t, d), wr_t, router_bias.reshape(-1, 1))
    e_chunk, n_active, tok_pad, dst_pad, n_assign = _dispatch(idx_t, d // 2 // LANES)
    yk = _moe(e_chunk, n_active, tok_pad, dst_pad, h2_packed,
              w1_experts.astype(BF16), w3_experts.astype(BF16), w2_experts.astype(BF16), n_assign)
    hs = _swiglu_up(h2, w1_shared.astype(BF16), w3_shared.astype(BF16))
    return _final(hs, w2_shared.astype(BF16), yk, wgt_t.T, x1, g2)


def kernel(x, c, ln1_gain, ln2_gain, w_ada, b_ada, w_in, moba_q_gain, moba_k_gain, dsa_q_gain, dsa_k_gain, idx_k_gain, w_proj_moba, w_proj_dsa, w_out, w_router, router_bias, w1_experts, w3_experts, w2_experts, w1_shared, w3_shared, w2_shared):
    depth = w_ada.shape[0]
    b = x.shape[0]
    c_pad = jnp.pad(c, ((0, (-b) % 8), (0, 0)))
    params = (ln1_gain, ln2_gain, w_ada, b_ada, w_in, moba_q_gain, moba_k_gain, dsa_q_gain, dsa_k_gain,
              idx_k_gain, w_proj_moba, w_proj_dsa, w_out, w_router, router_bias, w1_experts, w3_experts,
              w2_experts, w1_shared, w3_shared, w2_shared)
    for l in range(depth):
        x = _layer(x, c_pad, *[p[l] for p in params])
    return x
```

```python
import functools

import jax
import jax.numpy as jnp
from jax import lax
from jax.experimental import pallas as pl
from jax.experimental.pallas import tpu as pltpu

F32 = jnp.float32
BF16 = jnp.bfloat16
I32 = jnp.int32
U32 = jnp.uint32

HEAD_DIM = 128
MOBA_HEADS = 16
MOBA_BLOCK = 256
MOBA_TOPK = 3
DSA_HEADS = 16
DSA_KV_HEADS = 4
DSA_TOPK_MAX = 256
IDX_HEADS = 32
IDX_DIM = 64
N_EXPERTS = 64
EXPERT_TOPK = 8
N_GROUPS = 8
TOPK_GROUPS = 4
ROUTED_SCALE = 2.5
NORM_EPS = 1e-6

LANES = 128
V7X_VMEM_LIMIT_BYTES = 56 * 1024 * 1024

MOE_CHUNK = 256
YK_PACK_TILE = 512
NEG = -1e30
INT_MIN = -2 ** 31
LOG2E = 1.4426950408889634

_NT = (((1,), (1,)), ((), ()))


def _cparams(sem):
    return pltpu.CompilerParams(dimension_semantics=sem, vmem_limit_bytes=V7X_VMEM_LIMIT_BYTES)


def _tile(n, pref):
    if n <= pref:
        return n
    t = pref - pref % LANES
    while t > LANES and n % t:
        t -= LANES
    assert n % t == 0, (n, pref)
    return t


def _ada_kernel(c_ref, w_ref, b_ref, o_ref):
    c = c_ref[...]
    s = c * jax.nn.sigmoid(c)
    acc = jnp.dot(s.astype(BF16), w_ref[...].astype(BF16), preferred_element_type=F32)
    o_ref[...] = acc + b_ref[...]


def _ada(c_pad, w_ada, b_ada):
    rows, d = c_pad.shape
    n = w_ada.shape[1]
    tn = _tile(n, 512)
    return pl.pallas_call(
        _ada_kernel,
        grid=(n // tn,),
        in_specs=[pl.BlockSpec((rows, d), lambda j: (0, 0)),
                  pl.BlockSpec((d, tn), lambda j: (0, j)),
                  pl.BlockSpec((1, tn), lambda j: (0, j))],
        out_specs=pl.BlockSpec((rows, tn), lambda j: (0, j)),
        out_shape=jax.ShapeDtypeStruct((rows, n), F32),
        compiler_params=_cparams(("arbitrary",)),
        name="ada_mod",
    )(c_pad, w_ada, b_ada)


def _pack_bf16_pairs(v):
    n = v.shape[1] // 2
    bits = lax.bitcast_convert_type(v.astype(F32), U32)
    return (bits[:, :n] >> 16) | bits[:, n:]


def _unpack_bf16_pairs(w):
    lo = lax.bitcast_convert_type(w << 16, F32)
    hi = lax.bitcast_convert_type(w & jnp.uint32(0xFFFF0000), F32)
    return lo, hi


def _normmod_kernel(x_ref, gain_ref, sc_ref, sh_ref, *o_refs):
    x = x_ref[0]
    y = x * lax.rsqrt(jnp.mean(x * x, axis=-1, keepdims=True) + NORM_EPS) * gain_ref[...]
    h = (y * (1.0 + sc_ref[0]) + sh_ref[0]).astype(BF16)
    o_refs[0][0] = h
    if len(o_refs) > 1:
        words = _pack_bf16_pairs(h)
        ts = words.shape[0]
        slab_rows = words.shape[1] // LANES
        for a in range(slab_rows):
            o_refs[1][pl.ds(a, ts, stride=slab_rows), :] = words[:, a * LANES:(a + 1) * LANES]


def _normmod(x, gain, sc, sh, with_packed):
    b, s, d = x.shape
    ts = _tile(s, 256)
    blk = pl.BlockSpec((1, ts, d), lambda bi, i: (bi, i, 0))
    vec = pl.BlockSpec((1, 1, d), lambda bi, i: (bi, 0, 0))
    out_shape = [jax.ShapeDtypeStruct((b, s, d), BF16)]
    out_specs = [blk]
    if with_packed:
        slab_rows = d // 2 // LANES
        out_shape.append(jax.ShapeDtypeStruct((b * s * slab_rows, LANES), U32))
        out_specs.append(pl.BlockSpec((ts * slab_rows, LANES), lambda bi, i: (bi * (s // ts) + i, 0)))
    return pl.pallas_call(
        _normmod_kernel,
        grid=(b, s // ts),
        in_specs=[blk, pl.BlockSpec((1, d), lambda bi, i: (0, 0)), vec, vec],
        out_specs=out_specs,
        out_shape=out_shape,
        compiler_params=_cparams(("arbitrary", "arbitrary")),
        name="norm_modulate",
    )(x, gain.reshape(1, d), sc.reshape(b, 1, d), sh.reshape(b, 1, d))


def _proj_kernel(a_ref, w_ref, aux_ref, o_ref, *, mode):
    acc = jnp.dot(a_ref[0], w_ref[...], preferred_element_type=F32)
    tn = acc.shape[1]
    if mode == "plain":
        o_ref[0] = acc.astype(o_ref.dtype)
    elif mode == "sigmoid":
        o_ref[0] = jax.nn.sigmoid(acc).astype(o_ref.dtype)
    elif mode == "headnorm":
        for g in range(tn // HEAD_DIM):
            sl = slice(g * HEAD_DIM, (g + 1) * HEAD_DIM)
            blk = acc[:, sl]
            ms = jnp.mean(blk * blk, axis=-1, keepdims=True)
            o_ref[0, :, sl] = (blk * lax.rsqrt(ms + NORM_EPS) * aux_ref[:, sl]).astype(o_ref.dtype)
    elif mode == "indexer":
        lane = lax.broadcasted_iota(I32, acc.shape, 1)
        is_key = lane < IDX_DIM
        ms = jnp.sum(jnp.where(is_key, acc * acc, 0.0), axis=-1, keepdims=True) / IDX_DIM
        o_ref[0] = acc * jnp.where(is_key, lax.rsqrt(ms + NORM_EPS), 1.0) * aux_ref[...]
    else:
        raise ValueError(mode)


def _proj(a, w, aux, mode, out_dtype, tm_pref=1024, tn_pref=512):
    b, s, k = a.shape
    n = w.shape[1]
    tm, tn = _tile(s, tm_pref), _tile(n, tn_pref)
    return pl.pallas_call(
        functools.partial(_proj_kernel, mode=mode),
        grid=(b, s // tm, n // tn),
        in_specs=[pl.BlockSpec((1, tm, k), lambda bi, i, j: (bi, i, 0)),
                  pl.BlockSpec((k, tn), lambda bi, i, j: (0, j)),
                  pl.BlockSpec((1, tn), lambda bi, i, j: (0, j))],
        out_specs=pl.BlockSpec((1, tm, tn), lambda bi, i, j: (bi, i, j)),
        out_shape=jax.ShapeDtypeStruct((b, s, n), out_dtype),
        compiler_params=_cparams(("arbitrary", "arbitrary", "arbitrary")),
        name="proj_" + mode,
    )(a, w, aux)


AUG_ALIBI = 0
AUG_SEL = 8
MOBA_QBLOCKS = 4


def _moba_kernel(sp_ref, q_ref, k_ref, v_ref, o_ref, kaug_sc, vt_sc, kmean_sc, *, nb, gt, qb):
    h = pl.program_id(1)
    ip = pl.program_id(2)
    blk = MOBA_BLOCK
    tq = qb * blk
    nbp = kmean_sc.shape[0]
    gblocks = gt // blk

    @pl.when(ip == 0)
    def _():
        kmean_sc[...] = jnp.zeros_like(kmean_sc)
        lane = lax.broadcasted_iota(I32, (blk, LANES), 1)
        row = lax.broadcasted_iota(I32, (blk, LANES), 0)
        for n in range(nb):
            rows = slice(n * blk, (n + 1) * blk)
            kb = k_ref[0, rows, :]
            kmean_sc[n:n + 1, :] = jnp.mean(kb.astype(F32), axis=0, keepdims=True)
            aug = jnp.where(lane == AUG_SEL + n, 1.0, _alibi_key_aug(n * blk + row, lane))
            kaug_sc[rows, :HEAD_DIM] = kb
            kaug_sc[rows, HEAD_DIM:] = aug.astype(BF16)
            vt_sc[:, rows] = v_ref[0, rows, :].astype(F32).T.astype(BF16)

    q = q_ref[0]
    gate_t = lax.dot_general(kmean_sc[...], q.astype(F32), _NT, preferred_element_type=F32,
                             precision=lax.Precision.HIGHEST)
    r_iota = lax.broadcasted_iota(I32, gate_t.shape, 0)
    assert blk & (blk - 1) == 0
    own = ip * qb + (lax.broadcasted_iota(I32, gate_t.shape, 1) >> (blk.bit_length() - 1))
    rank = jnp.zeros(gate_t.shape, I32)
    for m in range(nb):
        gm = gate_t[m:m + 1, :]
        beats = ((gm > gate_t) | ((gm == gate_t) & (m < r_iota))) & (m < own)
        rank = rank + jnp.where(beats, 1, 0)
    keep_t = ((r_iota < own) & (rank < MOBA_TOPK)) | (r_iota == own)
    selb_t = jnp.where(keep_t, 0.0, NEG)

    augq_t = jnp.concatenate([_alibi_query_aug(sp_ref, h, (8, tq), 0), selb_t,
                              jnp.zeros((LANES - 8 - nbp, tq), F32)], axis=0)
    qa = jnp.concatenate([q, augq_t.T.astype(BF16)], axis=1)

    t_pos = ip * tq + lax.broadcasted_iota(I32, (gt, tq), 1)
    s_off = lax.broadcasted_iota(I32, (gt, tq), 0)

    def group(gi, carry, causal):
        m_prev, l_prev, acc_prev = carry
        g0 = pl.multiple_of(gi * gt, gt)
        st = lax.dot_general(kaug_sc[pl.ds(g0, gt), :], qa, _NT, preferred_element_type=F32)
        if causal:
            st = jnp.where(g0 + s_off <= t_pos, st, NEG)
        m_new = jnp.maximum(m_prev, jnp.max(st, axis=0, keepdims=True))
        alpha = jnp.exp2(m_prev - m_new)
        p = jnp.exp2(st - m_new)
        l_new = alpha * l_prev + jnp.sum(p, axis=0, keepdims=True)
        pv = jnp.dot(vt_sc[:, pl.ds(g0, gt)], p.astype(BF16), preferred_element_type=F32)
        return m_new, l_new, alpha * acc_prev + pv

    init = (jnp.full((1, tq), NEG, F32), jnp.zeros((1, tq), F32), jnp.zeros((HEAD_DIM, tq), F32))
    n_full = (ip * qb) // gblocks
    carry = lax.fori_loop(0, n_full, lambda gi, c: group(gi, c, False), init)
    _, l_fin, acc_fin = group(n_full, carry, True)
    o_ref[0] = (acc_fin / l_fin).T.astype(o_ref.dtype)


def _moba(slope_parts, qk, vq, q_col0, k_col0, v_col0):
    b, s, _ = qk.shape
    assert s % MOBA_BLOCK == 0 and s <= 4096
    nb = s // MOBA_BLOCK
    nbp = -(-nb // 8) * 8
    assert AUG_SEL + nbp <= LANES
    gt = min(s, 4 * MOBA_BLOCK)
    qb = MOBA_QBLOCKS
    assert s % gt == 0 and nb % qb == 0 and (gt // MOBA_BLOCK) % qb == 0
    tq = qb * MOBA_BLOCK
    return pl.pallas_call(
        functools.partial(_moba_kernel, nb=nb, gt=gt, qb=qb),
        grid=(b, MOBA_HEADS, nb // qb),
        in_specs=[pl.BlockSpec(memory_space=pltpu.SMEM),
                  pl.BlockSpec((1, tq, HEAD_DIM), lambda bi, h, i: (bi, i, q_col0 + h)),
                  pl.BlockSpec((1, s, HEAD_DIM), lambda bi, h, i: (bi, 0, k_col0 + h)),
                  pl.BlockSpec((1, s, HEAD_DIM), lambda bi, h, i: (bi, 0, v_col0 + h))],
        out_specs=pl.BlockSpec((1, tq, HEAD_DIM), lambda bi, h, i: (bi, i, h)),
        out_shape=jax.ShapeDtypeStruct((b, s, MOBA_HEADS * HEAD_DIM), BF16),
        scratch_shapes=[pltpu.VMEM((s, 2 * HEAD_DIM), BF16),
                        pltpu.VMEM((HEAD_DIM, s), BF16),
                        pltpu.VMEM((nbp, HEAD_DIM), F32)],
        compiler_params=_cparams(("arbitrary", "arbitrary", "arbitrary")),
        name="moba_attention",
    )(slope_parts, qk, qk, vq)


def _dsa_select_kernel(qi_ref, ki_ref, wi_ref, bias_ref, key_sc, qs_sc, *, tq, tk, nsel, hgroup):
    j = pl.program_id(1)
    t0 = j * tq
    nkc = (t0 + tq + tk - 1) // tk
    n_hg = IDX_HEADS // hgroup

    bias_ref[...] = jnp.full(bias_ref.shape, NEG, bias_ref.dtype)
    for hd in range(IDX_HEADS):
        qs_sc[hd * tq:(hd + 1) * tq, :] = qi_ref[0, :, hd * IDX_DIM:(hd + 1) * IDX_DIM]
    w = wi_ref[0]
    row_t = t0 + lax.broadcasted_iota(I32, (tq, tk), 0)
    col_i = lax.broadcasted_iota(I32, (tq, tk), 1)

    def score_chunk(c, _):
        c0 = pl.multiple_of(c * tk, tk)
        kc = ki_ref[0, pl.ds(c0, tk), :]
        acc = jnp.zeros((tq, tk), F32)
        for g in range(n_hg):
            qg = qs_sc[g * hgroup * tq:(g + 1) * hgroup * tq, :]
            d = lax.dot_general(qg, kc, _NT, preferred_element_type=F32)
            for hh in range(hgroup):
                hd = g * hgroup + hh
                acc = acc + w[:, hd:hd + 1] * jnp.maximum(d[hh * tq:(hh + 1) * tq], 0.0)
        bits = lax.bitcast_convert_type(acc, I32)
        key = jnp.where(bits >= 0, bits, bits ^ 0x7FFFFFFF)
        key = jnp.where(c0 + col_i <= row_t, key, INT_MIN)
        key_sc[:, pl.ds(c0, tk)] = key
        return 0

    lax.fori_loop(0, nkc, score_chunk, 0)

    def count_rows(pred):
        def cnt_chunk(c, cnt):
            c0 = pl.multiple_of(c * tk, tk)
            hit = jnp.where(pred(key_sc[:, pl.ds(c0, tk)], c0), 1.0, 0.0)
            for g in range(tk // LANES):
                cnt = cnt + hit[:, g * LANES:(g + 1) * LANES]
            return cnt

        cnt = lax.fori_loop(0, nkc, cnt_chunk, jnp.zeros((tq, LANES), F32))
        return jnp.sum(cnt, axis=-1, keepdims=True)

    def bit_body(bi, carry):
        prefix, cnt_prefix = carry
        cand = prefix + jnp.left_shift(jnp.int32(1), 31 - bi)
        total = count_rows(lambda blk, c0: blk >= cand)
        take = total >= nsel
        return jnp.where(take, cand, prefix), jnp.where(take, total, cnt_prefix)

    n_proc = jnp.zeros((tq, 1), F32) + (nkc * tk).astype(F32)
    thr, cnt_thr = lax.fori_loop(0, 32, bit_body, (jnp.full((tq, 1), INT_MIN, I32), n_proc))

    s_len = key_sc.shape[1]
    nbits = s_len.bit_length()

    def tie_bound():
        need = nsel - count_rows(lambda blk, c0: blk > thr)
        def jbit(bi, jb):
            cand = jb + jnp.left_shift(jnp.int32(1), nbits - 1 - bi)
            below = count_rows(lambda blk, c0: (blk == thr) & (c0 + col_i < cand) & (c0 + col_i <= row_t))
            return jnp.where(below <= need, cand, jb)
        return lax.fori_loop(0, nbits, jbit, jnp.zeros((tq, 1), I32))

    jbound = lax.cond(jnp.max(cnt_thr) > nsel, tie_bound, lambda: jnp.full((tq, 1), s_len, I32))

    def out_chunk(c, _):
        c0 = pl.multiple_of(c * tk, tk)
        blk = key_sc[:, pl.ds(c0, tk)]
        s_idx = c0 + col_i
        keep = ((blk > thr) | ((blk == thr) & (s_idx < jbound))) & (s_idx <= row_t)
        bias_ref[0, :, pl.ds(c0, tk)] = jnp.where(keep, 0.0, NEG).astype(bias_ref.dtype)
        return 0

    lax.fori_loop(0, nkc, out_chunk, 0)


def _dsa_select(vq, qi_blk, ki, wi):
    b, s, _ = vq.shape
    iw = IDX_HEADS * IDX_DIM
    tq = _tile(s, 128)
    tk = _tile(s, 512)
    nsel = min(DSA_TOPK_MAX, s // 4)
    return pl.pallas_call(
        functools.partial(_dsa_select_kernel, tq=tq, tk=tk, nsel=nsel, hgroup=8),
        grid=(b, s // tq),
        in_specs=[pl.BlockSpec((1, tq, iw), lambda bi, j: (bi, j, qi_blk)),
                  pl.BlockSpec((1, s, IDX_DIM), lambda bi, j: (bi, 0, 0)),
                  pl.BlockSpec((1, tq, IDX_HEADS), lambda bi, j: (bi, j, 0))],
        out_specs=pl.BlockSpec((1, tq, s), lambda bi, j: (bi, j, 0)),
        out_shape=jax.ShapeDtypeStruct((b, s, s), BF16),
        scratch_shapes=[pltpu.VMEM((tq, s), I32), pltpu.VMEM((IDX_HEADS * tq, IDX_DIM), BF16)],
        compiler_params=_cparams(("arbitrary", "arbitrary")),
        name="dsa_select",
    )(vq, ki, wi)


def _alibi_key_aug(pos, lane):
    alibi = jnp.where((lane & 1) == 0, ((pos >> 4) << 4).astype(F32), (pos & 15).astype(F32))
    return jnp.where(lane < AUG_ALIBI + 6, alibi, 0.0)


def _alibi_query_aug(sp_ref, h, shape, axis):
    ix = lax.broadcasted_iota(I32, shape, axis)
    return jnp.where(ix < 2, sp_ref[h, 0], jnp.where(ix < 4, sp_ref[h, 1], jnp.where(ix < 6, sp_ref[h, 2], 0.0)))


def _dsa_attn_kernel(sp_ref, q_ref, k_ref, v_ref, bias_ref, o_ref, kaug_sc, *, tq, tk, grp):
    g = pl.program_id(1)
    j = pl.program_id(2)
    t0 = j * tq
    nk = (t0 + tq + tk - 1) // tk
    rows = grp * tq
    dh = HEAD_DIM
    s_len = kaug_sc.shape[0]

    @pl.when(j == 0)
    def _():
        lane = lax.broadcasted_iota(I32, (tk, LANES), 1)
        row = lax.broadcasted_iota(I32, (tk, LANES), 0)
        for n in range(s_len // tk):
            rs = slice(n * tk, (n + 1) * tk)
            kaug_sc[rs, :dh] = k_ref[0, rs, :]
            kaug_sc[rs, dh:] = _alibi_key_aug(n * tk + row, lane).astype(BF16)

    q = jnp.concatenate(
        [jnp.concatenate([q_ref[0, :, r * dh:(r + 1) * dh],
                          _alibi_query_aug(sp_ref, g * grp + r, (tq, LANES), 1).astype(BF16)], axis=1)
         for r in range(grp)], axis=0)

    def body(c, carry):
        m_prev, l_prev, acc_prev = carry
        c0 = pl.multiple_of(c * tk, tk)
        s = lax.dot_general(q, kaug_sc[pl.ds(c0, tk), :], _NT, preferred_element_type=F32)
        mb = bias_ref[0, :, pl.ds(c0, tk)].astype(F32)
        lg = jnp.concatenate([s[r * tq:(r + 1) * tq] + mb for r in range(grp)], axis=0)
        m_new = jnp.maximum(m_prev, jnp.max(lg, axis=-1, keepdims=True))
        alpha = jnp.exp2(m_prev - m_new)
        p = jnp.exp2(lg - m_new)
        l_new = alpha * l_prev + jnp.sum(p, axis=-1, keepdims=True)
        pv = jnp.dot(p.astype(BF16), v_ref[0, pl.ds(c0, tk), :], preferred_element_type=F32)
        return m_new, l_new, alpha * acc_prev + pv

    init = (jnp.full((rows, 1), NEG, F32), jnp.zeros((rows, 1), F32), jnp.zeros((rows, dh), F32))
    _, l_fin, acc_fin = lax.fori_loop(0, nk, body, init)
    out = acc_fin / l_fin
    for r in range(grp):
        o_ref[0, :, r * dh:(r + 1) * dh] = out[r * tq:(r + 1) * tq].astype(o_ref.dtype)


def _dsa_attn(slope_parts, qk, vq, bias, q_col0, k_col0, v_col0):
    b, s, _ = qk.shape
    assert s <= 4096
    grp = DSA_HEADS // DSA_KV_HEADS
    gw = grp * HEAD_DIM
    assert (q_col0 * HEAD_DIM) % gw == 0
    q_blk0 = q_col0 * HEAD_DIM // gw
    tq = _tile(s, 256)
    tk = _tile(s, 512)
    return pl.pallas_call(
        functools.partial(_dsa_attn_kernel, tq=tq, tk=tk, grp=grp),
        grid=(b, DSA_KV_HEADS, s // tq),
        in_specs=[pl.BlockSpec(memory_space=pltpu.SMEM),
                  pl.BlockSpec((1, tq, gw), lambda bi, g, j: (bi, j, q_blk0 + g)),
                  pl.BlockSpec((1, s, HEAD_DIM), lambda bi, g, j: (bi, 0, k_col0 + g)),
                  pl.BlockSpec((1, s, HEAD_DIM), lambda bi, g, j: (bi, 0, v_col0 + g)),
                  pl.BlockSpec((1, tq, s), lambda bi, g, j: (bi, j, 0))],
        out_specs=pl.BlockSpec((1, tq, gw), lambda bi, g, j: (bi, j, g)),
        out_shape=jax.ShapeDtypeStruct((b, s, DSA_HEADS * HEAD_DIM), BF16),
        scratch_shapes=[pltpu.VMEM((s, 2 * HEAD_DIM), BF16)],
        compiler_params=_cparams(("arbitrary", "arbitrary", "arbitrary")),
        name="dsa_attention",
    )(slope_parts, qk, qk, vq, bias)


def _merge_kernel(oa_ref, wa_ref, ob_ref, wb_ref, ga_ref, gb_ref, o_ref):
    ya = jnp.dot(oa_ref[0], wa_ref[...], preferred_element_type=F32)
    yb = jnp.dot(ob_ref[0], wb_ref[...], preferred_element_type=F32)
    o_ref[0] = (ga_ref[0].astype(F32) * ya + gb_ref[0].astype(F32) * yb).astype(o_ref.dtype)


def _merge(oa, wa, ob, wb, gates):
    b, s, ka = oa.shape
    kb = ob.shape[2]
    n = wa.shape[1]
    tm, tn = _tile(s, 1024), _tile(n, 512)
    nj = n // tn
    return pl.pallas_call(
        _merge_kernel,
        grid=(b, s // tm, nj),
        in_specs=[pl.BlockSpec((1, tm, ka), lambda bi, i, j: (bi, i, 0)),
                  pl.BlockSpec((ka, tn), lambda bi, i, j: (0, j)),
                  pl.BlockSpec((1, tm, kb), lambda bi, i, j: (bi, i, 0)),
                  pl.BlockSpec((kb, tn), lambda bi, i, j: (0, j)),
                  pl.BlockSpec((1, tm, tn), lambda bi, i, j: (bi, i, j)),
                  pl.BlockSpec((1, tm, tn), lambda bi, i, j: (bi, i, nj + j))],
        out_specs=pl.BlockSpec((1, tm, tn), lambda bi, i, j: (bi, i, j)),
        out_shape=jax.ShapeDtypeStruct((b, s, n), BF16),
        compiler_params=_cparams(("arbitrary", "arbitrary", "arbitrary")),
        name="gated_merge",
    )(oa, wa, ob, wb, gates, gates)


def _resid_kernel(a_ref, w_ref, x_ref, g_ref, o_ref):
    y = jnp.dot(a_ref[0], w_ref[...], preferred_element_type=F32)
    o_ref[0] = x_ref[0] + g_ref[0] * y


def _resid_mm(a, w, x, g):
    b, s, k = a.shape
    n = w.shape[1]
    tm, tn = _tile(s, 1024), _tile(n, 512)
    return pl.pallas_call(
        _resid_kernel,
        grid=(b, s // tm, n // tn),
        in_specs=[pl.BlockSpec((1, tm, k), lambda bi, i, j: (bi, i, 0)),
                  pl.BlockSpec((k, tn), lambda bi, i, j: (0, j)),
                  pl.BlockSpec((1, tm, tn), lambda bi, i, j: (bi, i, j)),
                  pl.BlockSpec((1, 1, tn), lambda bi, i, j: (bi, 0, j))],
        out_specs=pl.BlockSpec((1, tm, tn), lambda bi, i, j: (bi, i, j)),
        out_shape=jax.ShapeDtypeStruct((b, s, n), F32),
        compiler_params=_cparams(("arbitrary", "arbitrary", "arbitrary")),
        name="out_proj_residual",
    )(a, w, x, g.reshape(b, 1, n))


def _swiglu_kernel(a_ref, w1_ref, w3_ref, o_ref):
    a = a_ref[0]
    u = jnp.dot(a, w1_ref[...], preferred_element_type=F32)
    v = jnp.dot(a, w3_ref[...], preferred_element_type=F32)
    o_ref[0] = (u * jax.nn.sigmoid(u) * v).astype(o_ref.dtype)


def _swiglu_up(a, w1, w3):
    b, s, k = a.shape
    n = w1.shape[1]
    tm, tn = _tile(s, 1024), _tile(n, 512)
    return pl.pallas_call(
        _swiglu_kernel,
        grid=(b, s // tm, n // tn),
        in_specs=[pl.BlockSpec((1, tm, k), lambda bi, i, j: (bi, i, 0)),
                  pl.BlockSpec((k, tn), lambda bi, i, j: (0, j)),
                  pl.BlockSpec((k, tn), lambda bi, i, j: (0, j))],
        out_specs=pl.BlockSpec((1, tm, tn), lambda bi, i, j: (bi, i, j)),
        out_shape=jax.ShapeDtypeStruct((b, s, n), BF16),
        compiler_params=_cparams(("arbitrary", "arbitrary", "arbitrary")),
        name="shared_swiglu_up",
    )(a, w1, w3)


def _final_kernel(a_ref, w_ref, *rest):
    yk_refs, (rw_ref, x_ref, g_ref, o_ref) = rest[:EXPERT_TOPK], rest[EXPERT_TOPK:]
    y = jnp.dot(a_ref[0], w_ref[...], preferred_element_type=F32)
    lo = hi = None
    for kk, r in enumerate(yk_refs):
        lo_r, hi_r = _unpack_bf16_pairs(r[...])
        wk = rw_ref[:, kk:kk + 1]
        lo = lo_r * wk if lo is None else lo + lo_r * wk
        hi = hi_r * wk if hi is None else hi + hi_r * wk
    o_ref[0] = x_ref[0] + g_ref[0] * (y + jnp.concatenate([lo, hi], axis=1))


def _final(hs, w2, yk, route_w, x, g):
    b, s, k = hs.shape
    n = w2.shape[1]
    tm, tn = _tile(s, 512), YK_PACK_TILE
    assert n % tn == 0
    nti = s // tm
    t_blocks = b * nti
    yk_specs = [pl.BlockSpec((tm, tn // 2), lambda bi, i, j, kk=kk: (kk * t_blocks + bi * nti + i, j))
                for kk in range(EXPERT_TOPK)]
    return pl.pallas_call(
        _final_kernel,
        grid=(b, nti, n // tn),
        in_specs=[pl.BlockSpec((1, tm, k), lambda bi, i, j: (bi, i, 0)),
                  pl.BlockSpec((k, tn), lambda bi, i, j: (0, j))] + yk_specs + [
                  pl.BlockSpec((tm, EXPERT_TOPK), lambda bi, i, j: (bi * nti + i, 0)),
                  pl.BlockSpec((1, tm, tn), lambda bi, i, j: (bi, i, j)),
                  pl.BlockSpec((1, 1, tn), lambda bi, i, j: (bi, 0, j))],
        out_specs=pl.BlockSpec((1, tm, tn), lambda bi, i, j: (bi, i, j)),
        out_shape=jax.ShapeDtypeStruct((b, s, n), F32),
        compiler_params=_cparams(("arbitrary", "arbitrary", "arbitrary")),
        name="moe_combine_residual",
    )(hs, w2, *([yk] * EXPERT_TOPK), route_w, x, g.reshape(b, 1, n))


def _router_kernel(h_ref, wr_ref, b_ref, idx_ref, wgt_ref):
    tm = h_ref.shape[0]
    per = N_EXPERTS // N_GROUPS
    logits = lax.dot_general(wr_ref[...], h_ref[...], _NT, preferred_element_type=F32)
    scores = jax.nn.sigmoid(logits)
    choice = scores + b_ref[...]
    sub = lax.broadcasted_iota(I32, (per, tm), 0).astype(F32)
    gs = []
    for g in range(N_GROUPS):
        cg = choice[g * per:(g + 1) * per]
        m1 = jnp.max(cg, axis=0, keepdims=True)
        i1 = jnp.min(jnp.where(cg == m1, sub, float(per)), axis=0, keepdims=True)
        m2 = jnp.max(jnp.where(sub == i1, -jnp.inf, cg), axis=0, keepdims=True)
        gs.append(m1 + m2)
    masked = []
    for g in range(N_GROUPS):
        rank = jnp.zeros((1, tm), F32)
        for g2 in range(N_GROUPS):
            if g2 == g:
                continue
            beats = (gs[g2] > gs[g]) | ((gs[g2] == gs[g]) & (g2 < g))
            rank = rank + jnp.where(beats, 1.0, 0.0)
        masked.append(jnp.where(rank < TOPK_GROUPS, choice[g * per:(g + 1) * per], -jnp.inf))
    cm = jnp.concatenate(masked, axis=0)
    erow = lax.broadcasted_iota(I32, (N_EXPERTS, tm), 0).astype(F32)
    idxs, wts = [], []
    for _ in range(EXPERT_TOPK):
        mx = jnp.max(cm, axis=0, keepdims=True)
        ix = jnp.min(jnp.where(cm == mx, erow, float(N_EXPERTS)), axis=0, keepdims=True)
        hit = erow == ix
        idxs.append(ix)
        wts.append(jnp.sum(jnp.where(hit, scores, 0.0), axis=0, keepdims=True))
        cm = jnp.where(hit, -jnp.inf, cm)
    wt = jnp.concatenate(wts, axis=0)
    idx_ref[...] = jnp.concatenate(idxs, axis=0).astype(I32)
    wgt_ref[...] = wt / jnp.sum(wt, axis=0, keepdims=True) * ROUTED_SCALE


def _router(h2_flat, wr_t, bias_col):
    t, d = h2_flat.shape
    tm = _tile(t, 512)
    return pl.pallas_call(
        _router_kernel,
        grid=(t // tm,),
        in_specs=[pl.BlockSpec((tm, d), lambda i: (i, 0)),
                  pl.BlockSpec((N_EXPERTS, d), lambda i: (0, 0)),
                  pl.BlockSpec((N_EXPERTS, 1), lambda i: (0, 0))],
        out_specs=[pl.BlockSpec((EXPERT_TOPK, tm), lambda i: (0, i)),
                   pl.BlockSpec((EXPERT_TOPK, tm), lambda i: (0, i))],
        out_shape=[jax.ShapeDtypeStruct((EXPERT_TOPK, t), I32),
                   jax.ShapeDtypeStruct((EXPERT_TOPK, t), F32)],
        compiler_params=_cparams(("arbitrary",)),
        name="router_topk",
    )(h2_flat, wr_t, bias_col)


MOE_DMA_UNROLL = 8
DMA_PRIORITIES = 2


def _moe_kernel(ech_ref, nact_ref, tokc_ref, tokn_ref, dst_ref, h_hbm, w1_ref, w3_ref, w2_ref,
                yk_hbm, xs, ys, gsem, ssem, *, dump_row0):
    del ech_ref
    c = pl.program_id(0)
    nc = pl.num_programs(0)
    nact = nact_ref[0]
    rows = MOE_CHUNK
    sl = xs.shape[1] // rows
    slot = c % 2

    def gather_copy(tok_ref, r, s):
        return pltpu.make_async_copy(h_hbm.at[pl.ds(pl.multiple_of(tok_ref[0, 0, r], sl), sl), :],
                                     xs.at[s, pl.ds(pl.multiple_of(r * sl, sl), sl), :], gsem.at[s])

    def scatter_copy(r, dst_row):
        return pltpu.make_async_copy(ys.at[pl.ds(r, 1), :], yk_hbm.at[pl.ds(dst_row, 1), :], ssem.at[0])

    def per_row(fn):
        def body(g, carry):
            for j in range(MOE_DMA_UNROLL):
                fn(g * MOE_DMA_UNROLL + j, j % DMA_PRIORITIES)
            return carry
        lax.fori_loop(0, rows // MOE_DMA_UNROLL, body, 0)

    def wait_scatter():
        per_row(lambda r, p: scatter_copy(r, 0).wait())

    @pl.when(c == 0)
    def _():
        ys[...] = jnp.zeros_like(ys)
        per_row(lambda r, p: scatter_copy(r, dump_row0 + r).start(priority=p))
        per_row(lambda r, p: gather_copy(tokc_ref, r, 0).start(priority=p))

    @pl.when(c < nact)
    def _():
        per_row(lambda r, p: gather_copy(tokn_ref, r, 1 - slot).start(priority=p))
        per_row(lambda r, p: gather_copy(tokc_ref, r, slot).wait())

        xsl = xs.at[slot]
        parts = [_unpack_bf16_pairs(xsl[pl.ds(a, rows, stride=sl), :]) for a in range(sl)]
        x = jnp.concatenate([p[0] for p in parts] + [p[1] for p in parts], axis=1).astype(BF16)
        u = jnp.dot(x, w1_ref[0], preferred_element_type=F32)
        v = jnp.dot(x, w3_ref[0], preferred_element_type=F32)
        hm = (u * jax.nn.sigmoid(u) * v).astype(BF16)
        d = w2_ref.shape[2]
        half = YK_PACK_TILE // 2
        tiles = []
        for j in range(d // YK_PACK_TILE):
            yj = jnp.dot(hm, w2_ref[0, :, j * YK_PACK_TILE:(j + 1) * YK_PACK_TILE],
                         preferred_element_type=F32)
            tiles.append(_pack_bf16_pairs(yj.astype(BF16)))
        wait_scatter()
        for j, tile in enumerate(tiles):
            ys[:, j * half:(j + 1) * half] = tile
        per_row(lambda r, p: scatter_copy(r, dst_ref[0, 0, r]).start(priority=p))

    @pl.when(c == nc - 1)
    def _():
        wait_scatter()
        per_row(lambda r, p: gather_copy(tokc_ref, r, nact % 2).wait())


def _moe(e_chunk, n_active, tok_pad, dst_pad, h2_slabs, w1, w3, w2, n_assign):
    n_chunks = tok_pad.shape[0]
    d, f = w1.shape[1], w1.shape[2]
    dp = d // 2
    sl = dp // LANES
    assert dp % LANES == 0 and d % YK_PACK_TILE == 0 and h2_slabs.shape[1] == LANES
    rows = MOE_CHUNK
    smem_blk = lambda fn: pl.BlockSpec((1, 1, rows), fn, memory_space=pltpu.SMEM)
    grid_spec = pltpu.PrefetchScalarGridSpec(
        num_scalar_prefetch=2,
        grid=(n_chunks,),
        in_specs=[smem_blk(lambda c, ech, na: (c, 0, 0)),
                  smem_blk(lambda c, ech, na: (jnp.minimum(c + 1, n_chunks - 1), 0, 0)),
                  smem_blk(lambda c, ech, na: (c, 0, 0)),
                  pl.BlockSpec(memory_space=pl.ANY),
                  pl.BlockSpec((1, d, f), lambda c, ech, na: (ech[c], 0, 0)),
                  pl.BlockSpec((1, d, f), lambda c, ech, na: (ech[c], 0, 0)),
                  pl.BlockSpec((1, f, d), lambda c, ech, na: (ech[c], 0, 0))],
        out_specs=pl.BlockSpec(memory_space=pl.ANY),
        scratch_shapes=[pltpu.VMEM((2, rows * sl, LANES), U32),
                        pltpu.VMEM((rows, dp), U32),
                        pltpu.SemaphoreType.DMA((2,)),
                        pltpu.SemaphoreType.DMA((1,))],
    )
    return pl.pallas_call(
        functools.partial(_moe_kernel, dump_row0=n_assign),
        grid_spec=grid_spec,
        out_shape=jax.ShapeDtypeStruct((n_assign + rows, dp), U32),
        compiler_params=_cparams(("arbitrary",)),
        name="routed_experts",
    )(e_chunk, n_active, tok_pad, tok_pad, dst_pad, h2_slabs, w1, w3, w2)


def _dispatch(idx_t, slab_rows):
    k, t = idx_t.shape
    e, c = N_EXPERTS, MOE_CHUNK
    m = k * t
    n_chunks = m // c + e
    e_flat = idx_t.reshape(m)
    order = jnp.argsort(e_flat).astype(I32)
    counts = jnp.sum((e_flat[None, :] == jnp.arange(e, dtype=I32)[:, None]).astype(I32), axis=1)
    starts = jnp.cumsum(counts) - counts
    padded = (counts + c - 1) // c * c
    pends = jnp.cumsum(padded)
    pstarts = pends - padded
    c_start = jnp.arange(n_chunks, dtype=I32) * c
    e_chunk = jnp.minimum(jnp.sum((c_start[:, None] >= pends[None, :]).astype(I32), axis=1), e - 1)
    r = jnp.arange(c, dtype=I32)[None, :]
    off = (c_start - pstarts[e_chunk])[:, None] + r
    valid = (off < counts[e_chunk][:, None]) & (c_start < pends[-1])[:, None]
    a_p = order[jnp.where(valid, starts[e_chunk][:, None] + off, 0)]
    tok_pad = (jnp.where(valid, a_p % t, 0) * slab_rows).astype(I32)
    dst_pad = jnp.where(valid, a_p, m + r).astype(I32)
    n_active = (pends[-1] // c).astype(I32).reshape(1)
    return e_chunk.astype(I32), n_active, tok_pad[:, None, :], dst_pad[:, None, :], m


def _alibi_slopes(n_heads):
    return 2.0 ** (-8.0 * jnp.arange(1, n_heads + 1, dtype=F32) / n_heads)


def _bf16_parts(v):
    p1 = v.astype(BF16).astype(F32)
    p2 = (v - p1).astype(BF16).astype(F32)
    p3 = (v - p1 - p2).astype(BF16).astype(F32)
    return jnp.stack([p1, p2, p3], axis=1)


def _layer(x, c_pad, ln1_gain, ln2_gain, w_ada, b_ada, w_in, moba_q_gain, moba_k_gain, dsa_q_gain,
           dsa_k_gain, idx_k_gain, w_proj_moba, w_proj_dsa, w_out, w_router, router_bias,
           w1_experts, w3_experts, w2_experts, w1_shared, w3_shared, w2_shared):
    b, s, d = x.shape
    t = b * s
    mw = MOBA_HEADS * HEAD_DIM
    qw = DSA_HEADS * HEAD_DIM
    kvw = DSA_KV_HEADS * HEAD_DIM
    iw = IDX_HEADS * IDX_DIM

    mod = _ada(c_pad, w_ada, b_ada.reshape(1, -1))[:b]
    sh1, sc1, g1, sh2, sc2, g2 = jnp.split(mod, 6, axis=-1)

    (h,) = _normmod(x, ln1_gain, sc1, sh1, with_packed=False)

    o_qa, o_ka, o_va = 0, mw, 2 * mw
    o_qb = 3 * mw
    o_kb = o_qb + qw
    o_vb = o_kb + kvw
    o_qi = o_vb + kvw
    o_ki = o_qi + iw
    o_wi = o_ki + IDX_DIM
    o_ga = o_wi + IDX_HEADS
    o_gb = o_ga + d
    cols = lambda a, n: w_in[:, a:a + n]

    scale = HEAD_DIM ** -0.5 * LOG2E
    w_qk = jnp.concatenate([cols(o_qa, mw), cols(o_ka, mw), cols(o_qb, qw), cols(o_kb, kvw)], axis=1).astype(BF16)
    g_qk = jnp.concatenate([jnp.tile(moba_q_gain * scale, MOBA_HEADS), jnp.tile(moba_k_gain, MOBA_HEADS),
                            jnp.tile(dsa_q_gain * scale, DSA_HEADS), jnp.tile(dsa_k_gain, DSA_KV_HEADS)])
    qk = _proj(h, w_qk, g_qk.reshape(1, -1), "headnorm", BF16)
    w_vq = jnp.concatenate([cols(o_qi, iw), cols(o_va, mw), cols(o_vb, kvw)], axis=1).astype(BF16)
    vq = _proj(h, w_vq, jnp.zeros((1, w_vq.shape[1]), F32), "plain", BF16)
    w_gate = cols(o_ga, 2 * d).astype(BF16)
    gates = _proj(h, w_gate, jnp.zeros((1, 2 * d), F32), "sigmoid", BF16)
    w_ix = jnp.pad(cols(o_ki, IDX_DIM + IDX_HEADS), ((0, 0), (0, LANES - IDX_DIM - IDX_HEADS))).astype(BF16)
    aux_ix = jnp.concatenate([idx_k_gain, jnp.full((IDX_HEADS,), IDX_HEADS ** -0.5 * IDX_DIM ** -0.5, F32),
                              jnp.zeros((LANES - IDX_DIM - IDX_HEADS,), F32)])
    kiw = _proj(h, w_ix, aux_ix.reshape(1, -1), "indexer", F32, tn_pref=LANES)

    hd = HEAD_DIM
    o_a = _moba(_bf16_parts(_alibi_slopes(MOBA_HEADS) * LOG2E), qk, vq, q_col0=0, k_col0=mw // hd, v_col0=iw // hd)

    ki = kiw[:, :, :IDX_DIM].astype(BF16)
    wi = kiw[:, :, IDX_DIM:IDX_DIM + IDX_HEADS]
    bias = _dsa_select(vq, 0, ki, wi)
    o_b = _dsa_attn(_bf16_parts(_alibi_slopes(DSA_HEADS) * LOG2E), qk, vq, bias, q_col0=2 * mw // hd,
                    k_col0=(2 * mw + qw) // hd, v_col0=(iw + mw) // hd)

    merged = _merge(o_a, w_proj_moba.astype(BF16), o_b, w_proj_dsa.astype(BF16), gates)
    x1 = _resid_mm(merged, w_out.astype(BF16), x, g1)

    h2, h2_packed = _normmod(x1, ln2_gain, sc2, sh2, with_packed=True)
    wr_t = w_router.T.astype(BF16)
    idx_t, wgt_t = _router(h2.reshape(t, d), wr_t, router_bias.reshape(-1, 1))
    e_chunk, n_active, tok_pad, dst_pad, n_assign = _dispatch(idx_t, d // 2 // LANES)
    yk = _moe(e_chunk, n_active, tok_pad, dst_pad, h2_packed,
              w1_experts.astype(BF16), w3_experts.astype(BF16), w2_experts.astype(BF16), n_assign)
    hs = _swiglu_up(h2, w1_shared.astype(BF16), w3_shared.astype(BF16))
    return _final(hs, w2_shared.astype(BF16), yk, wgt_t.T, x1, g2)


def kernel(x, c, ln1_gain, ln2_gain, w_ada, b_ada, w_in, moba_q_gain, moba_k_gain, dsa_q_gain, dsa_k_gain, idx_k_gain, w_proj_moba, w_proj_dsa, w_out, w_router, router_bias, w1_experts, w3_experts, w2_experts, w1_shared, w3_shared, w2_shared):
    depth = w_ada.shape[0]
    b = x.shape[0]
    c_pad = jnp.pad(c, ((0, (-b) % 8), (0, 0)))
    params = (ln1_gain, ln2_gain, w_ada, b_ada, w_in, moba_q_gain, moba_k_gain, dsa_q_gain, dsa_k_gain,
              idx_k_gain, w_proj_moba, w_proj_dsa, w_out, w_router, router_bias, w1_experts, w3_experts,
              w2_experts, w1_shared, w3_shared, w2_shared)
    for l in range(depth):
        x = _layer(x, c_pad, *[p[l] for p in params])
    return x
```
